```python
import math
import jax, jax.numpy as jnp
from jax import lax
import numpy as np

D_MODEL = 1024
BATCH = 2
SEQ = 8192
DEPTH = 2
DEC_BATCH = 32
DEC_SEQ = 1
PAST_LEN = 8192
PAGE_SIZE = 128

N_MIXERS = 2
N_DIFF_LAYERS = (DEPTH + 1) // 2
N_MOBA_LAYERS = DEPTH // 2
DIFF_HEADS = 8
DIFF_DK = D_MODEL // DIFF_HEADS // 2
DIFF_DV = 2 * DIFF_DK
MOBA_HEADS = 16
MOBA_DH = D_MODEL // MOBA_HEADS
MOBA_BLOCK = 256
MOBA_TOPK = 3
N_EXPERTS = 32
TOP_K = 4
D_FF = D_MODEL
SWIGLU_LIMIT = 7.0
SWIGLU_ALPHA = 1.702
ROPE_THETA = 10000.0
LN_EPS = 1e-5
DEEPNORM_ALPHA = (2.0 * DEPTH) ** 0.25
DEEPNORM_BETA = (8.0 * DEPTH) ** -0.25
Q_BLOCK = 128
MOBA_Q_CHUNK = 64
MOE_BLOCK = 128

kernel_name = 'hybrid_diffattn_moba_moe_decode_step'

F32 = jnp.float32


def layer_norm(x, g, b):
    xf = x.astype(F32)
    mu = jnp.mean(xf, -1, keepdims=True)
    var = jnp.mean(jnp.square(xf - mu), -1, keepdims=True)
    return ((xf - mu) * lax.rsqrt(var + LN_EPS) * g + b).astype(x.dtype)


def rope(x, pos):
    d = x.shape[-1]
    inv = ROPE_THETA ** (-jnp.arange(0, d, 2, dtype=F32) / d)
    ang = pos.astype(F32)[:, None] * inv[None, :]
    ang = jnp.concatenate([ang, ang], -1)
    shape = (1, pos.shape[0]) + (1,) * (x.ndim - 3) + (d,)
    cos = jnp.cos(ang).reshape(shape)
    sin = jnp.sin(ang).reshape(shape)
    xf = x.astype(F32)
    x1, x2 = xf[..., : d // 2], xf[..., d // 2:]
    rot = jnp.concatenate([-x2, x1], -1)
    return (xf * cos + rot * sin).astype(x.dtype)


def gather_pages(pool, layer, page_table):
    rows = pool[layer, page_table]
    return rows.reshape((rows.shape[0], rows.shape[1] * rows.shape[2]) + rows.shape[3:])


def diff_project(x, pos, w_qkv):
    B, L, _ = x.shape
    q, k, v = jnp.split(x @ w_qkv, 3, axis=-1)
    q = rope(q.reshape(B, L, DIFF_HEADS, 2, DIFF_DK), pos)
    k = rope(k.reshape(B, L, DIFF_HEADS, 2, DIFF_DK), pos)
    v = v.reshape(B, L, DIFF_HEADS, DIFF_DV)
    return q, k, v


def diff_attend(q, k, v, q_pos, k_pos, lam):
    s = jnp.einsum('bqhcd,bthcd->bhcqt', q, k).astype(F32) / math.sqrt(DIFF_DK)
    s = jnp.where(k_pos[None, :] <= q_pos[:, None], s, -jnp.inf)
    p = jax.nn.softmax(s, axis=-1)
    a = p[:, :, 0] - lam * p[:, :, 1]
    return jnp.einsum('bhqt,bthe->bqhe', a.astype(v.dtype), v)


def diff_mixer_prompt(x, pos, w_qkv, lam):
    q, k, v = diff_project(x, pos, w_qkv)
    B, L = x.shape[:2]
    nqb = L // Q_BLOCK
    qb = jnp.moveaxis(q.reshape((B, nqb, Q_BLOCK) + q.shape[2:]), 1, 0)
    pb = pos.reshape(nqb, Q_BLOCK)
    o = lax.map(lambda a: diff_attend(a[0], k, v, a[1], pos, lam), (qb, pb))
    o = jnp.moveaxis(o, 0, 1).reshape(B, L, DIFF_HEADS, DIFF_DV)
    return o, k, v


def diff_mixer_sample(x, pos, cache_k, cache_v, layer, page_table, w_qkv, lam):
    q, k, v = diff_project(x, pos, w_qkv)
    k_all = jnp.concatenate([gather_pages(cache_k, layer, page_table), k], axis=1)
    v_all = jnp.concatenate([gather_pages(cache_v, layer, page_table), v], axis=1)
    k_pos = jnp.arange(k_all.shape[1], dtype=jnp.int32)
    o = diff_attend(q, k_all, v_all, pos, k_pos, lam)
    return o, k, v


def diff_out(o, g, lam_init, w_o):
    B, L = o.shape[:2]
    of = o.astype(F32)
    of = of * lax.rsqrt(jnp.mean(of * of, -1, keepdims=True) + LN_EPS) * g * (1.0 - lam_init)
    return of.astype(o.dtype).reshape(B, L, DIFF_HEADS * DIFF_DV) @ w_o


def moba_project(x, pos, w_qkv):
    B, L, _ = x.shape
    q, k, v = jnp.split(x @ w_qkv, 3, axis=-1)
    q = rope(q.reshape(B, L, MOBA_HEADS, MOBA_DH), pos)
    k = rope(k.reshape(B, L, MOBA_HEADS, MOBA_DH), pos)
    v = v.reshape(B, L, MOBA_HEADS, MOBA_DH)
    return q, k, v


def moba_blocks(k, v):
    B, T = k.shape[:2]
    nb = -(-T // MOBA_BLOCK)
    padw = ((0, 0), (0, nb * MOBA_BLOCK - T), (0, 0), (0, 0))
    kb = jnp.pad(k, padw).reshape(B, nb, MOBA_BLOCK, MOBA_HEADS, MOBA_DH)
    vb = jnp.pad(v, padw).reshape(B, nb, MOBA_BLOCK, MOBA_HEADS, MOBA_DH)
    kmean = jnp.mean(kb.astype(F32), axis=2).astype(k.dtype)
    return kb, vb, kmean


def moba_attend(q, q_pos, kb, vb, kmean):
    B, Q, H, dh = q.shape
    nb = kb.shape[1]
    own = (q_pos // MOBA_BLOCK).astype(jnp.int32)
    g = jnp.einsum('bqhd,bnhd->bhqn', q, kmean).astype(F32)
    g = jnp.where(jnp.arange(nb)[None, :] < own[:, None], g, -jnp.inf)
    _, sel = lax.top_k(g, min(MOBA_TOPK, nb))
    own_b = jnp.broadcast_to(own[None, None, :, None], (B, H, Q, 1)).astype(sel.dtype)
    idx = jnp.concatenate([sel, own_b], axis=-1)
    slot_ok = jnp.concatenate([sel < own_b, jnp.ones(own_b.shape, bool)], axis=-1)
    bi = jnp.arange(B)[:, None, None, None]
    hi = jnp.arange(H)[None, :, None, None]
    k_sel = kb[bi, idx, :, hi]
    v_sel = vb[bi, idx, :, hi]
    key_pos = idx[..., None] * MOBA_BLOCK + jnp.arange(MOBA_BLOCK, dtype=idx.dtype)
    mask = slot_ok[..., None] & (key_pos <= q_pos[None, None, :, None, None])
    s = jnp.einsum('bqhd,bhqnkd->bhqnk', q, k_sel).astype(F32) / math.sqrt(dh)
    s = jnp.where(mask, s, -jnp.inf).reshape(B, H, Q, -1)
    p = jax.nn.softmax(s, axis=-1).reshape(mask.shape)
    return jnp.einsum('bhqnk,bhqnkd->bqhd', p.astype(v_sel.dtype), v_sel)


def moba_mixer_prompt(x, pos, w_qkv):
    q, k, v = moba_project(x, pos, w_qkv)
    B, L = x.shape[:2]
    kb, vb, kmean = moba_blocks(k, v)
    nqc = L // MOBA_Q_CHUNK
    qc = jnp.moveaxis(q.reshape(B, nqc, MOBA_Q_CHUNK, MOBA_HEADS, MOBA_DH), 1, 0)
    pc = pos.reshape(nqc, MOBA_Q_CHUNK)
    o = lax.map(lambda a: moba_attend(a[0], a[1], kb, vb, kmean), (qc, pc))
    o = jnp.moveaxis(o, 0, 1).reshape(B, L, MOBA_HEADS * MOBA_DH)
    return o, k, v


def moba_mixer_sample(x, pos, cache_k, cache_v, layer, page_table, w_qkv):
    q, k, v = moba_project(x, pos, w_qkv)
    B, L = x.shape[:2]
    k_all = jnp.concatenate([gather_pages(cache_k, layer, page_table), k], axis=1)
    v_all = jnp.concatenate([gather_pages(cache_v, layer, page_table), v], axis=1)
    kb, vb, kmean = moba_blocks(k_all, v_all)
    o = moba_attend(q, pos, kb, vb, kmean).reshape(B, L, MOBA_HEADS * MOBA_DH)
    return o, k, v


def expert_ffn(xb, w_gu, b_gu, w_dn, b_dn):
    h = xb @ w_gu + b_gu
    gate = jnp.minimum(h[..., :D_FF], SWIGLU_LIMIT)
    up = jnp.clip(h[..., D_FF:], -SWIGLU_LIMIT, SWIGLU_LIMIT)
    glu = gate * jax.nn.sigmoid(SWIGLU_ALPHA * gate)
    return ((up + 1.0) * glu) @ w_dn + b_dn


def moe_ffn(x, w_r, b_r, w_gu, b_gu, w_dn, b_dn):
    shp = x.shape
    x2 = x.reshape(-1, shp[-1])
    n = x2.shape[0]
    logits = (x2 @ w_r + b_r).astype(F32)
    top_val, top_idx = lax.top_k(logits, TOP_K)
    gates = jax.nn.softmax(top_val, axis=-1).astype(x.dtype)
    n_asg = n * TOP_K
    flat_e = top_idx.reshape(-1)
    flat_tok = jnp.arange(n_asg, dtype=jnp.int32) // TOP_K
    order = jnp.argsort(flat_e)
    sorted_e = flat_e[order]
    counts = jnp.bincount(flat_e, length=N_EXPERTS)
    padded = (counts + MOE_BLOCK - 1) // MOE_BLOCK * MOE_BLOCK
    ends_p = jnp.cumsum(padded)
    start_p = ends_p - padded
    start = jnp.cumsum(counts) - counts
    dest = start_p[sorted_e] + (jnp.arange(n_asg, dtype=jnp.int32) - start[sorted_e])
    n_blocks = -(-n_asg // MOE_BLOCK) + N_EXPERTS
    n_rows = n_blocks * MOE_BLOCK
    row_tok = jnp.full((n_rows,), n, jnp.int32).at[dest].set(flat_tok[order])
    row_gate = jnp.zeros((n_rows,), x.dtype).at[dest].set(gates.reshape(-1)[order])
    blk_e = jnp.minimum(jnp.searchsorted(ends_p, jnp.arange(n_blocks) * MOE_BLOCK, side='right'), N_EXPERTS - 1)
    x_pad = jnp.concatenate([x2, jnp.zeros((1, shp[-1]), x.dtype)], axis=0)
    xb = x_pad[row_tok].reshape(n_blocks, MOE_BLOCK, shp[-1])

    def run_block(args):
        xi, e = args
        return expert_ffn(xi, w_gu[e], b_gu[e], w_dn[e], b_dn[e])

    yb = lax.map(run_block, (xb, blk_e)).reshape(n_rows, shp[-1])
    y = jax.ops.segment_sum(yb * row_gate[:, None], row_tok, num_segments=n + 1)[:n]
    return y.reshape(shp)


def setup_inputs(seed: int = 0) -> dict:
    key = jax.random.key(seed)
    ks = jax.random.split(key, 32)
    n_pages = PAST_LEN // PAGE_SIZE
    n_used = DEC_BATCH * n_pages
    n_pool = (n_used * 5) // 4
    s = D_MODEL ** -0.5
    beta = DEEPNORM_BETA

    def nrm(k, shape, scale):
        return jax.random.normal(k, shape, F32) * scale

    x_prompt = nrm(ks[0], (BATCH, SEQ, D_MODEL), 1.0)
    x_sample = nrm(ks[1], (DEC_BATCH, DEC_SEQ, D_MODEL), 1.0)
    cache_k_diff = nrm(ks[2], (N_DIFF_LAYERS, n_pool, PAGE_SIZE, DIFF_HEADS, 2, DIFF_DK), 1.0)
    cache_v_diff = nrm(ks[3], (N_DIFF_LAYERS, n_pool, PAGE_SIZE, DIFF_HEADS, DIFF_DV), beta)
    cache_k_moba = nrm(ks[4], (N_MOBA_LAYERS, n_pool, PAGE_SIZE, MOBA_HEADS, MOBA_DH), 1.0)
    cache_v_moba = nrm(ks[5], (N_MOBA_LAYERS, n_pool, PAGE_SIZE, MOBA_HEADS, MOBA_DH), beta)
    page_table = jax.random.permutation(ks[6], n_pool)[:n_used].reshape(DEC_BATCH, n_pages).astype(jnp.int32)
    diff_w_qkv = nrm(ks[7], (N_DIFF_LAYERS, D_MODEL, 3 * D_MODEL), s)
    diff_w_qkv = diff_w_qkv.at[:, :, 2 * D_MODEL:].multiply(beta)
    diff_w_o = nrm(ks[8], (N_DIFF_LAYERS, D_MODEL, D_MODEL), s * beta)
    diff_lambda_q1 = nrm(ks[9], (N_DIFF_LAYERS, DIFF_DK), 0.1)
    diff_lambda_k1 = nrm(ks[10], (N_DIFF_LAYERS, DIFF_DK), 0.1)
    diff_lambda_q2 = nrm(ks[11], (N_DIFF_LAYERS, DIFF_DK), 0.1)
    diff_lambda_k2 = nrm(ks[12], (N_DIFF_LAYERS, DIFF_DK), 0.1)
    diff_subln_g = 1.0 + nrm(ks[13], (N_DIFF_LAYERS, DIFF_DV), 0.02)
    moba_w_qkv = nrm(ks[14], (N_MOBA_LAYERS, D_MODEL, 3 * D_MODEL), s)
    moba_w_qkv = moba_w_qkv.at[:, :, 2 * D_MODEL:].multiply(beta)
    moba_w_o = nrm(ks[15], (N_MOBA_LAYERS, D_MODEL, D_MODEL), s * beta)
    ln1_g = 1.0 + nrm(ks[16], (DEPTH, D_MODEL), 0.02)
    ln1_b = nrm(ks[17], (DEPTH, D_MODEL), 0.02)
    ln2_g = 1.0 + nrm(ks[18], (DEPTH, D_MODEL), 0.02)
    ln2_b = nrm(ks[19], (DEPTH, D_MODEL), 0.02)
    moe_w_router = nrm(ks[20], (DEPTH, D_MODEL, N_EXPERTS), s)
    moe_b_router = nrm(ks[21], (DEPTH, N_EXPERTS), 0.01)
    moe_w_gate_up = nrm(ks[22], (DEPTH, N_EXPERTS, D_MODEL, 2 * D_FF), s * beta)
    moe_b_gate_up = nrm(ks[23], (DEPTH, N_EXPERTS, 2 * D_FF), 0.02)
    moe_w_down = nrm(ks[24], (DEPTH, N_EXPERTS, D_FF, D_MODEL), (D_FF ** -0.5) * beta)
    moe_b_down = nrm(ks[25], (DEPTH, N_EXPERTS, D_MODEL), 0.02)
    return {'x_prompt': x_prompt, 'x_sample': x_sample,
            'cache_k_diff': cache_k_diff, 'cache_v_diff': cache_v_diff,
            'cache_k_moba': cache_k_moba, 'cache_v_moba': cache_v_moba,
            'page_table': page_table,
            'diff_w_qkv': diff_w_qkv, 'diff_w_o': diff_w_o,
            'diff_lambda_q1': diff_lambda_q1, 'diff_lambda_k1': diff_lambda_k1,
            'diff_lambda_q2': diff_lambda_q2, 'diff_lambda_k2': diff_lambda_k2,
            'diff_subln_g': diff_subln_g,
            'moba_w_qkv': moba_w_qkv, 'moba_w_o': moba_w_o,
            'ln1_g': ln1_g, 'ln1_b': ln1_b, 'ln2_g': ln2_g, 'ln2_b': ln2_b,
            'moe_w_router': moe_w_router, 'moe_b_router': moe_b_router,
            'moe_w_gate_up': moe_w_gate_up, 'moe_b_gate_up': moe_b_gate_up,
            'moe_w_down': moe_w_down, 'moe_b_down': moe_b_down}


def reference(x_prompt, x_sample, cache_k_diff, cache_v_diff, cache_k_moba, cache_v_moba,
              page_table, diff_w_qkv, diff_w_o, diff_lambda_q1, diff_lambda_k1,
              diff_lambda_q2, diff_lambda_k2, diff_subln_g, moba_w_qkv, moba_w_o,
              ln1_g, ln1_b, ln2_g, ln2_b, moe_w_router, moe_b_router,
              moe_w_gate_up, moe_b_gate_up, moe_w_down, moe_b_down):
    pos_p = jnp.arange(x_prompt.shape[1], dtype=jnp.int32)
    pos_s = PAST_LEN + jnp.arange(x_sample.shape[1], dtype=jnp.int32)
    h_p, h_s = x_prompt, x_sample
    kd_p, vd_p, km_p, vm_p = [], [], [], []
    kd_s, vd_s, km_s, vm_s = [], [], [], []
    for i in range(DEPTH):
        j = i // N_MIXERS
        if i % N_MIXERS == 0:
            lam_init = 0.8 - 0.6 * math.exp(-0.3 * i)
            lam = (jnp.exp(jnp.sum(diff_lambda_q1[j].astype(F32) * diff_lambda_k1[j].astype(F32)))
                   - jnp.exp(jnp.sum(diff_lambda_q2[j].astype(F32) * diff_lambda_k2[j].astype(F32)))
                   + lam_init)
            o_p, k_p, v_p = diff_mixer_prompt(h_p, pos_p, diff_w_qkv[j], lam)
            o_s, k_s, v_s = diff_mixer_sample(h_s, pos_s, cache_k_diff, cache_v_diff, j,
                                              page_table, diff_w_qkv[j], lam)
            mix_p = diff_out(o_p, diff_subln_g[j], lam_init, diff_w_o[j])
            mix_s = diff_out(o_s, diff_subln_g[j], lam_init, diff_w_o[j])
            kd_p.append(k_p)
            vd_p.append(v_p)
            kd_s.append(k_s)
            vd_s.append(v_s)
        else:
            o_p, k_p, v_p = moba_mixer_prompt(h_p, pos_p, moba_w_qkv[j])
            o_s, k_s, v_s = moba_mixer_sample(h_s, pos_s, cache_k_moba, cache_v_moba, j,
                                              page_table, moba_w_qkv[j])
            mix_p = o_p @ moba_w_o[j]
            mix_s = o_s @ moba_w_o[j]
            km_p.append(k_p)
            vm_p.append(v_p)
            km_s.append(k_s)
            vm_s.append(v_s)
        h_p = layer_norm(DEEPNORM_ALPHA * h_p + mix_p, ln1_g[i], ln1_b[i])
        h_s = layer_norm(DEEPNORM_ALPHA * h_s + mix_s, ln1_g[i], ln1_b[i])
        ffn_p = moe_ffn(h_p, moe_w_router[i], moe_b_router[i], moe_w_gate_up[i],
                        moe_b_gate_up[i], moe_w_down[i], moe_b_down[i])
        ffn_s = moe_ffn(h_s, moe_w_router[i], moe_b_router[i], moe_w_gate_up[i],
                        moe_b_gate_up[i], moe_w_down[i], moe_b_down[i])
        h_p = layer_norm(DEEPNORM_ALPHA * h_p + ffn_p, ln2_g[i], ln2_b[i])
        h_s = layer_norm(DEEPNORM_ALPHA * h_s + ffn_s, ln2_g[i], ln2_b[i])
    y_prompt = h_p
    y_sample = h_s
    new_k_diff_prompt = jnp.stack(kd_p)
    new_v_diff_prompt = jnp.stack(vd_p)
    new_k_moba_prompt = jnp.stack(km_p)
    new_v_moba_prompt = jnp.stack(vm_p)
    new_k_diff_sample = jnp.stack(kd_s)
    new_v_diff_sample = jnp.stack(vd_s)
    new_k_moba_sample = jnp.stack(km_s)
    new_v_moba_sample = jnp.stack(vm_s)
    return (y_prompt, y_sample, new_k_diff_prompt, new_v_diff_prompt, new_k_moba_prompt,
            new_v_moba_prompt, new_k_diff_sample, new_v_diff_sample, new_k_moba_sample,
            new_v_moba_sample)
```

```python
import functools
import math

import jax
import jax.numpy as jnp
from jax import lax
from jax.experimental import pallas as pl
from jax.experimental.pallas import tpu as pltpu

F32 = jnp.float32
BF16 = jnp.bfloat16
NEG_INF = float("-inf")

D_MODEL = 1024
PAGE_SIZE = 128
DIFF_HEADS = 8
DIFF_DK = 64
MOBA_HEADS = 16
MOBA_DH = 64
MOBA_BLOCK = 256
MOBA_TOPK = 3
N_EXPERTS = 32
TOP_K = 4
D_FF = D_MODEL
SWIGLU_LIMIT = 7.0
SWIGLU_ALPHA = 1.702
ROPE_THETA = 10000.0
LN_EPS = 1e-5
HEAD_DIM = 64
Q_SCALE = 1.0 / math.sqrt(HEAD_DIM)

LANES = 128
N_COLBLK = D_MODEL // LANES
TOK_TILE = 256
DIFF_TQ = 512
MOE_TILE = 256
DECODE_PAGES = 4
VMEM_LIMIT = 48 * 1024 * 1024


def _cparams(sem):
    return pltpu.CompilerParams(dimension_semantics=sem, vmem_limit_bytes=VMEM_LIMIT)


def _rope_tables(pos):
    inv = ROPE_THETA ** (-jnp.arange(0, HEAD_DIM, 2, dtype=F32) / HEAD_DIM)
    ang = pos.astype(F32)[:, None] * inv[None, :]
    ang = jnp.tile(ang, (1, LANES // (HEAD_DIM // 2)))
    cos, sin = jnp.cos(ang), jnp.sin(ang)
    first = (jnp.arange(LANES) % HEAD_DIM) < HEAD_DIM // 2
    return cos, jnp.where(first, -sin, 0.0), jnp.where(first, 0.0, sin)


def _proj_kernel(x_ref, w_ref, cos_ref, sa_ref, sb_ref,
                 q_ref, k_ref, v_ref, kb_ref, vb_ref, km_ref):
    xb = x_ref[...].astype(BF16)
    cos, sa, sb = cos_ref[...], sa_ref[...], sb_ref[...]

    def rope(a):
        half = HEAD_DIM // 2
        return a * cos + pltpu.roll(a, LANES - half, 1) * sa + pltpu.roll(a, half, 1) * sb

    aq = jnp.dot(xb, w_ref[:, 0:D_MODEL], preferred_element_type=F32)
    for c in range(N_COLBLK):
        sl = slice(c * LANES, (c + 1) * LANES)
        q_ref[:, sl] = (rope(aq[:, sl]) * Q_SCALE).astype(BF16)
    ak = jnp.dot(xb, w_ref[:, D_MODEL:2 * D_MODEL], preferred_element_type=F32)
    for c in range(N_COLBLK):
        sl = slice(c * LANES, (c + 1) * LANES)
        kr = rope(ak[:, sl])
        k_ref[:, sl] = kr
        kb_ref[:, sl] = kr.astype(BF16)
        km_ref[0, :, sl] = jnp.mean(kr, axis=0, keepdims=True)
    av = jnp.dot(xb, w_ref[:, 2 * D_MODEL:3 * D_MODEL], preferred_element_type=F32)
    v_ref[...] = av
    vb_ref[...] = av.astype(BF16)


def _proj_call(x, w_bf, cos, sa, sb):
    n = x.shape[0]
    nt = n // TOK_TILE
    row = lambda i: (i, 0)
    fix = lambda i: (0, 0)
    return pl.pallas_call(
        _proj_kernel,
        grid=(nt,),
        in_specs=[pl.BlockSpec((TOK_TILE, D_MODEL), row),
                  pl.BlockSpec((D_MODEL, 3 * D_MODEL), fix),
                  pl.BlockSpec((TOK_TILE, LANES), row),
                  pl.BlockSpec((TOK_TILE, LANES), row),
                  pl.BlockSpec((TOK_TILE, LANES), row)],
        out_specs=[pl.BlockSpec((TOK_TILE, D_MODEL), row),
                   pl.BlockSpec((TOK_TILE, D_MODEL), row),
                   pl.BlockSpec((TOK_TILE, D_MODEL), row),
                   pl.BlockSpec((TOK_TILE, D_MODEL), row),
                   pl.BlockSpec((TOK_TILE, D_MODEL), row),
                   pl.BlockSpec((1, 1, D_MODEL), lambda i: (i, 0, 0))],
        out_shape=[jax.ShapeDtypeStruct((n, D_MODEL), BF16),
                   jax.ShapeDtypeStruct((n, D_MODEL), F32),
                   jax.ShapeDtypeStruct((n, D_MODEL), F32),
                   jax.ShapeDtypeStruct((n, D_MODEL), BF16),
                   jax.ShapeDtypeStruct((n, D_MODEL), BF16),
                   jax.ShapeDtypeStruct((nt, 1, D_MODEL), F32)],
        compiler_params=_cparams(("arbitrary",)),
        name="qkv_rope",
    )(x, w_bf, cos, sa, sb)


def _half_masks(q):
    lane = lax.broadcasted_iota(jnp.int32, q.shape, 1)
    qf = q.astype(F32)
    return (jnp.where(lane < HEAD_DIM, qf, 0.0).astype(q.dtype),
            jnp.where(lane >= HEAD_DIM, qf, 0.0).astype(q.dtype))


def _dot_nt(a, b):
    return lax.dot_general(a, b, (((1,), (1,)), ((), ())), preferred_element_type=F32)


def _flash_update(c, s, vt, m_ref, l_ref, acc_ref):
    m_prev = m_ref[c]
    m_new = jnp.maximum(m_prev, jnp.max(s, axis=1, keepdims=True))
    alpha = jnp.exp(m_prev - m_new)
    p = jnp.exp(s - m_new)
    l_ref[c] = alpha * l_ref[c] + jnp.sum(p, axis=1, keepdims=True)
    acc_ref[c] = alpha * acc_ref[c] + jnp.dot(p.astype(BF16), vt, preferred_element_type=F32)
    m_ref[c] = m_new


def _diff_lambda(lq1, lk1, lq2, lk2, lam_init):
    return (jnp.exp(jnp.sum(lq1 * lk1, axis=1, keepdims=True))
            - jnp.exp(jnp.sum(lq2 * lk2, axis=1, keepdims=True)) + lam_init)


def _diff_attn_kernel(q_ref, k_ref, v_ref, lq1_ref, lk1_ref, lq2_ref, lk2_ref, g_ref,
                      o_ref, m_ref, l_ref, acc_ref, *, lam_init):
    t = DIFF_TQ
    qi = pl.program_id(2)
    q_maps = _half_masks(q_ref[...])
    m_ref[...] = jnp.full(m_ref.shape, NEG_INF, F32)
    l_ref[...] = jnp.zeros(l_ref.shape, F32)
    acc_ref[...] = jnp.zeros(acc_ref.shape, F32)

    def tile(kj, causal):
        start = pl.multiple_of(kj * t, t)
        kt = k_ref[pl.ds(start, t), :]
        vt = v_ref[pl.ds(start, t), :]
        for c in range(2):
            s = _dot_nt(q_maps[c], kt)
            if causal:
                row = lax.broadcasted_iota(jnp.int32, s.shape, 0)
                col = lax.broadcasted_iota(jnp.int32, s.shape, 1)
                s = jnp.where(col <= row, s, NEG_INF)
            _flash_update(c, s, vt, m_ref, l_ref, acc_ref)

    def body(kj, carry):
        tile(kj, False)
        return carry

    lax.fori_loop(0, qi, body, 0)
    tile(qi, True)

    lam = _diff_lambda(lq1_ref[...], lk1_ref[...], lq2_ref[...], lk2_ref[...], lam_init)
    o = acc_ref[0] / l_ref[0] - lam * (acc_ref[1] / l_ref[1])
    ms = jnp.mean(o * o, axis=1, keepdims=True)
    o = o * lax.rsqrt(ms + LN_EPS) * g_ref[...] * (1.0 - lam_init)
    o_ref[...] = o.astype(BF16)


def _diff_attn_call(q_bf, k_bf, v_bf, lams, g, batch, seq, lam_init):
    t = DIFF_TQ
    nq = seq // t
    qmap = lambda b, h, i: (b * nq + i, h)
    kvmap = lambda b, h, i: (b, h)
    fix = lambda b, h, i: (0, 0)
    return pl.pallas_call(
        functools.partial(_diff_attn_kernel, lam_init=lam_init),
        grid=(batch, DIFF_HEADS, nq),
        in_specs=[pl.BlockSpec((t, LANES), qmap),
                  pl.BlockSpec((seq, LANES), kvmap),
                  pl.BlockSpec((seq, LANES), kvmap)]
                 + [pl.BlockSpec((1, DIFF_DK), fix)] * 4
                 + [pl.BlockSpec((1, LANES), fix)],
        out_specs=pl.BlockSpec((t, LANES), qmap),
        out_shape=jax.ShapeDtypeStruct((batch * seq, D_MODEL), BF16),
        scratch_shapes=[pltpu.VMEM((2, t, 1), F32), pltpu.VMEM((2, t, 1), F32),
                        pltpu.VMEM((2, t, LANES), F32)],
        compiler_params=_cparams(("arbitrary", "arbitrary", "arbitrary")),
        name="diff_attn_prompt",
    )(q_bf, k_bf, v_bf, *lams, g)


def _row_query(q, width):
    rows = D_MODEL // width
    lane = lax.broadcasted_iota(jnp.int32, (rows, D_MODEL), 1)
    row = lax.broadcasted_iota(jnp.int32, (rows, D_MODEL), 0)
    qb = jnp.broadcast_to(q.astype(F32), (rows, D_MODEL))
    return jnp.where(lane // width == row, qb, 0.0).astype(q.dtype)


def _diff_decode_kernel(pt_ref, q_ref, kn_ref, vn_ref, *rest, lam_init):
    np_ = DECODE_PAGES
    k_refs, v_refs = rest[:np_], rest[np_:2 * np_]
    lq1_ref, lk1_ref, lq2_ref, lk2_ref, g_ref, o_ref, m_ref, l_ref, acc_ref = rest[2 * np_:]
    g = pl.program_id(1)
    qm = _row_query(q_ref[0], DIFF_DK)

    @pl.when(g == 0)
    def _():
        s_self = jnp.sum(qm.astype(F32) * kn_ref[0].astype(F32), axis=1, keepdims=True)
        m_ref[...] = s_self
        l_ref[...] = jnp.ones(l_ref.shape, F32)
        acc_ref[...] = jnp.broadcast_to(vn_ref[0].astype(F32), acc_ref.shape)

    s = jnp.concatenate([_dot_nt(qm, k_refs[i][...].astype(BF16)) for i in range(np_)], axis=1)
    m_prev = m_ref[...]
    m_new = jnp.maximum(m_prev, jnp.max(s, axis=1, keepdims=True))
    alpha = jnp.exp(m_prev - m_new)
    p = jnp.exp(s - m_new)
    l_ref[...] = alpha * l_ref[...] + jnp.sum(p, axis=1, keepdims=True)
    p = p.astype(BF16)
    pv = jnp.dot(p[:, 0:PAGE_SIZE], v_refs[0][...].astype(BF16), preferred_element_type=F32)
    for i in range(1, np_):
        pv += jnp.dot(p[:, i * PAGE_SIZE:(i + 1) * PAGE_SIZE], v_refs[i][...].astype(BF16),
                      preferred_element_type=F32)
    acc_ref[...] = alpha * acc_ref[...] + pv
    m_ref[...] = m_new

    @pl.when(g == pl.num_programs(1) - 1)
    def _():
        accn = acc_ref[...] / l_ref[...]
        row = lax.broadcasted_iota(jnp.int32, accn.shape, 0)
        lane = lax.broadcasted_iota(jnp.int32, accn.shape, 1)
        own = (lane // LANES) == (row // 2)
        zero = jnp.zeros_like(accn)
        o0 = jnp.sum(jnp.where(own & (row % 2 == 0), accn, zero), axis=0, keepdims=True)
        o1 = jnp.sum(jnp.where(own & (row % 2 == 1), accn, zero), axis=0, keepdims=True)
        lam = _diff_lambda(lq1_ref[...], lk1_ref[...], lq2_ref[...], lk2_ref[...], lam_init)
        o = o0 - lam * o1
        hrow = lax.broadcasted_iota(jnp.int32, (DIFF_HEADS, D_MODEL), 0)
        hlane = lax.broadcasted_iota(jnp.int32, (DIFF_HEADS, D_MODEL), 1)
        mine = (hlane // LANES) == hrow
        o8 = jnp.where(mine, jnp.broadcast_to(o, mine.shape), 0.0)
        ms = jnp.sum(o8 * o8, axis=1, keepdims=True) * (1.0 / LANES)
        scale8 = jnp.where(mine, jnp.broadcast_to(lax.rsqrt(ms + LN_EPS), mine.shape), 0.0)
        scale = jnp.sum(scale8, axis=0, keepdims=True)
        o_ref[0] = (o * scale * g_ref[...] * (1.0 - lam_init)).astype(BF16)


def _diff_decode_call(page_ids, q_s, k_new, v_new, cache_k, cache_v, lams, g_tiled, lam_init):
    db, n_pages = page_ids.shape
    np_ = DECODE_PAGES
    seq3 = lambda b, g, pt: (b, 0, 0)
    fix = lambda b, g, pt: (0, 0)

    def page(i):
        return pl.BlockSpec((None, PAGE_SIZE, D_MODEL), lambda b, g, pt: (pt[b, g * np_ + i], 0, 0))

    grid_spec = pltpu.PrefetchScalarGridSpec(
        num_scalar_prefetch=1,
        grid=(db, n_pages // np_),
        in_specs=[pl.BlockSpec((1, 1, D_MODEL), seq3)] * 3
                 + [page(i) for i in range(np_)] + [page(i) for i in range(np_)]
                 + [pl.BlockSpec((1, DIFF_DK), fix)] * 4
                 + [pl.BlockSpec((1, D_MODEL), fix)],
        out_specs=pl.BlockSpec((1, 1, D_MODEL), seq3),
        scratch_shapes=[pltpu.VMEM((2 * DIFF_HEADS, 1), F32), pltpu.VMEM((2 * DIFF_HEADS, 1), F32),
                        pltpu.VMEM((2 * DIFF_HEADS, D_MODEL), F32)],
    )
    return pl.pallas_call(
        functools.partial(_diff_decode_kernel, lam_init=lam_init),
        grid_spec=grid_spec,
        out_shape=jax.ShapeDtypeStruct((db, 1, D_MODEL), BF16),
        compiler_params=_cparams(("arbitrary", "arbitrary")),
        name="diff_attn_decode",
    )(page_ids, q_s, k_new, v_new, *([cache_k] * np_), *([cache_v] * np_), *lams, g_tiled)


def _top_blocks(g, blk, n_valid):
    g = jnp.where(blk < n_valid, g, NEG_INF)
    n_blk = g.shape[1]
    sel = jnp.zeros(g.shape, F32)
    for _ in range(MOBA_TOPK):
        mx = jnp.max(g, axis=1, keepdims=True)
        ix = jnp.min(jnp.where(g == mx, blk, n_blk), axis=1, keepdims=True)
        pick = blk == ix
        sel = jnp.where(pick & (ix < n_valid), 1.0, sel)
        g = jnp.where(pick, NEG_INF, g)
    return sel


def _moba_attn_kernel(q_ref, k_ref, v_ref, km_ref, o_ref, m_ref, l_ref, acc_ref, sel_ref):
    t = MOBA_BLOCK
    qi = pl.program_id(2)
    q_heads = _half_masks(q_ref[...])
    km = km_ref[...].astype(BF16)
    blk = lax.broadcasted_iota(jnp.int32, (t, km.shape[0]), 1)
    for c in range(2):
        sel_ref[c] = _top_blocks(_dot_nt(q_heads[c], km), blk, qi)
    m_ref[...] = jnp.full(m_ref.shape, NEG_INF, F32)
    l_ref[...] = jnp.zeros(l_ref.shape, F32)
    acc_ref[...] = jnp.zeros(acc_ref.shape, F32)

    def tile(kj, own):
        start = pl.multiple_of(kj * t, t)
        kt = k_ref[pl.ds(start, t), :]
        vt = v_ref[pl.ds(start, t), :]
        for c in range(2):
            s = _dot_nt(q_heads[c], kt)
            if own:
                row = lax.broadcasted_iota(jnp.int32, s.shape, 0)
                col = lax.broadcasted_iota(jnp.int32, s.shape, 1)
                s = jnp.where(col <= row, s, NEG_INF)
            else:
                chosen = jnp.max(jnp.where(blk == kj, sel_ref[c], 0.0), axis=1, keepdims=True)
                s = jnp.where(chosen > 0.0, s, NEG_INF)
            _flash_update(c, s, vt, m_ref, l_ref, acc_ref)

    tile(qi, True)

    def body(kj, carry):
        tile(kj, False)
        return carry

    lax.fori_loop(0, qi, body, 0)
    lane = lax.broadcasted_iota(jnp.int32, (t, LANES), 1)
    o = jnp.where(lane < HEAD_DIM, acc_ref[0] / l_ref[0], acc_ref[1] / l_ref[1])
    o_ref[...] = o.astype(BF16)


def _moba_attn_call(q_bf, k_bf, v_bf, kmean, batch, seq):
    t = MOBA_BLOCK
    nq = seq // t
    qmap = lambda b, h, i: (b * nq + i, h)
    kvmap = lambda b, h, i: (b, h)
    return pl.pallas_call(
        _moba_attn_kernel,
        grid=(batch, MOBA_HEADS // 2, nq),
        in_specs=[pl.BlockSpec((t, LANES), qmap),
                  pl.BlockSpec((seq, LANES), kvmap),
                  pl.BlockSpec((seq, LANES), kvmap),
                  pl.BlockSpec((nq, LANES), kvmap)],
        out_specs=pl.BlockSpec((t, LANES), qmap),
        out_shape=jax.ShapeDtypeStruct((batch * seq, D_MODEL), BF16),
        scratch_shapes=[pltpu.VMEM((2, t, 1), F32), pltpu.VMEM((2, t, 1), F32),
                        pltpu.VMEM((2, t, LANES), F32), pltpu.VMEM((2, t, nq), F32)],
        compiler_params=_cparams(("arbitrary", "arbitrary", "arbitrary")),
        name="moba_attn_prompt",
    )(q_bf, k_bf, v_bf, kmean)


def _moba_decode_kernel(pt_ref, q_ref, kn_ref, vn_ref, k0_ref, k1_ref, v0_ref, v1_ref,
                        o_ref, g_ref, m_ref, l_ref, part_ref):
    n = pl.program_id(1)
    n_blk = pl.num_programs(1)
    qm = _row_query(q_ref[0], MOBA_DH)
    col = lax.broadcasted_iota(jnp.int32, g_ref.shape, 1)
    k0, k1 = k0_ref[...], k1_ref[...]
    kmean = ((jnp.sum(k0, axis=0, keepdims=True) + jnp.sum(k1, axis=0, keepdims=True))
             * (1.0 / MOBA_BLOCK)).astype(BF16)
    gate = jnp.sum(qm.astype(F32) * kmean.astype(F32), axis=1, keepdims=True)

    @pl.when(n == 0)
    def _():
        g_ref[...] = jnp.full(g_ref.shape, NEG_INF, F32)
        m_ref[...] = jnp.zeros(m_ref.shape, F32)
        l_ref[...] = jnp.zeros(l_ref.shape, F32)

    s = jnp.concatenate([_dot_nt(qm, k0.astype(BF16)), _dot_nt(qm, k1.astype(BF16))], axis=1)
    m_blk = jnp.max(s, axis=1, keepdims=True)
    p = jnp.exp(s - m_blk)
    l_blk = jnp.sum(p, axis=1, keepdims=True)
    p = p.astype(BF16)
    part_ref[n] = (jnp.dot(p[:, 0:PAGE_SIZE], v0_ref[...].astype(BF16), preferred_element_type=F32)
                   + jnp.dot(p[:, PAGE_SIZE:], v1_ref[...].astype(BF16), preferred_element_type=F32))
    here = col == n
    g_ref[...] = jnp.where(here, gate, g_ref[...])
    m_ref[...] = jnp.where(here, m_blk, m_ref[...])
    l_ref[...] = jnp.where(here, l_blk, l_ref[...])

    @pl.when(n == n_blk - 1)
    def _():
        sel = _top_blocks(g_ref[...], col, n_blk)
        s_self = jnp.sum(qm.astype(F32) * kn_ref[0].astype(F32), axis=1, keepdims=True)
        m_all = m_ref[...]
        m_fin = jnp.maximum(jnp.max(jnp.where(sel > 0.0, m_all, NEG_INF), axis=1, keepdims=True), s_self)
        w = jnp.where(sel > 0.0, jnp.exp(m_all - m_fin), 0.0)
        w_self = jnp.exp(s_self - m_fin)
        l_fin = jnp.sum(w * l_ref[...], axis=1, keepdims=True) + w_self
        acc = w_self * jnp.broadcast_to(vn_ref[0].astype(F32), (MOBA_HEADS, D_MODEL))

        def add(j, a):
            wj = jnp.sum(jnp.where(col == j, w, 0.0), axis=1, keepdims=True)
            return a + wj * part_ref[j]

        acc = lax.fori_loop(0, n_blk, add, acc)
        accn = acc / l_fin
        row = lax.broadcasted_iota(jnp.int32, accn.shape, 0)
        lane = lax.broadcasted_iota(jnp.int32, accn.shape, 1)
        o = jnp.sum(jnp.where(lane // MOBA_DH == row, accn, 0.0), axis=0, keepdims=True)
        o_ref[0] = o.astype(BF16)


def _moba_decode_call(page_ids, q_s, k_new, v_new, cache_k, cache_v):
    db, n_pages = page_ids.shape
    n_blk = n_pages * PAGE_SIZE // MOBA_BLOCK
    seq3 = lambda b, n, pt: (b, 0, 0)

    def page(i):
        return pl.BlockSpec((None, PAGE_SIZE, D_MODEL), lambda b, n, pt: (pt[b, 2 * n + i], 0, 0))

    grid_spec = pltpu.PrefetchScalarGridSpec(
        num_scalar_prefetch=1,
        grid=(db, n_blk),
        in_specs=[pl.BlockSpec((1, 1, D_MODEL), seq3)] * 3 + [page(0), page(1), page(0), page(1)],
        out_specs=pl.BlockSpec((1, 1, D_MODEL), seq3),
        scratch_shapes=[pltpu.VMEM((MOBA_HEADS, LANES), F32), pltpu.VMEM((MOBA_HEADS, LANES), F32),
                        pltpu.VMEM((MOBA_HEADS, LANES), F32),
                        pltpu.VMEM((n_blk, MOBA_HEADS, D_MODEL), F32)],
    )
    return pl.pallas_call(
        _moba_decode_kernel,
        grid_spec=grid_spec,
        out_shape=jax.ShapeDtypeStruct((db, 1, D_MODEL), BF16),
        compiler_params=_cparams(("arbitrary", "arbitrary")),
        name="moba_attn_decode",
    )(page_ids, q_s, k_new, v_new, cache_k, cache_k, cache_v, cache_v)


def _layer_norm(z, g, b):
    mu = jnp.mean(z, axis=1, keepdims=True)
    zc = z - mu
    var = jnp.mean(zc * zc, axis=1, keepdims=True)
    return zc * lax.rsqrt(var + LN_EPS) * g + b


def _outproj_kernel(o_ref, h_ref, wo_ref, g_ref, b_ref, wr_ref, br_ref,
                    h1_ref, idx_ref, gate_ref, *, alpha):
    mix = jnp.dot(o_ref[...], wo_ref[...], preferred_element_type=F32)
    h1 = _layer_norm(alpha * h_ref[...] + mix, g_ref[...], b_ref[...])
    h1_ref[...] = h1
    logits = jnp.dot(h1, wr_ref[...], preferred_element_type=F32,
                     precision=lax.Precision.HIGHEST) + br_ref[...]
    lane = lax.broadcasted_iota(jnp.int32, logits.shape, 1)
    logits = jnp.where(lane < N_EXPERTS, logits, NEG_INF)
    vals, idxs = [], []
    for _ in range(TOP_K):
        mx = jnp.max(logits, axis=1, keepdims=True)
        ix = jnp.min(jnp.where(logits == mx, lane, LANES), axis=1, keepdims=True)
        vals.append(mx)
        idxs.append(ix)
        logits = jnp.where(lane == ix, NEG_INF, logits)
    es = [jnp.exp(v - vals[0]) for v in vals]
    tot = es[0] + es[1] + es[2] + es[3]
    for j in range(TOP_K):
        idx_ref[:, j:j + 1] = idxs[j]
        gate_ref[:, j:j + 1] = es[j] / tot


def _outproj_call(o_bf, h, wo_bf, ln_g, ln_b, wr_pad, br_pad, alpha):
    n = h.shape[0]
    row = lambda i: (i, 0)
    fix = lambda i: (0, 0)
    return pl.pallas_call(
        functools.partial(_outproj_kernel, alpha=alpha),
        grid=(n // TOK_TILE,),
        in_specs=[pl.BlockSpec((TOK_TILE, D_MODEL), row),
                  pl.BlockSpec((TOK_TILE, D_MODEL), row),
                  pl.BlockSpec((D_MODEL, D_MODEL), fix),
                  pl.BlockSpec((1, D_MODEL), fix),
                  pl.BlockSpec((1, D_MODEL), fix),
                  pl.BlockSpec((D_MODEL, LANES), fix),
                  pl.BlockSpec((1, LANES), fix)],
        out_specs=[pl.BlockSpec((TOK_TILE, D_MODEL), row),
                   pl.BlockSpec((TOK_TILE, TOP_K), row),
                   pl.BlockSpec((TOK_TILE, TOP_K), row)],
        out_shape=[jax.ShapeDtypeStruct((n, D_MODEL), F32),
                   jax.ShapeDtypeStruct((n, TOP_K), jnp.int32),
                   jax.ShapeDtypeStruct((n, TOP_K), F32)],
        compiler_params=_cparams(("arbitrary",)),
        name="outproj_ln_router",
    )(o_bf, h, wo_bf, ln_g, ln_b, wr_pad, br_pad)


def _row_copy(src_hbm, dst_vmem, sem, src_row, dst_row):
    return pltpu.make_async_copy(src_hbm.at[pl.ds(src_row, 1)], dst_vmem.at[pl.ds(dst_row, 1)], sem)


def _moe_kernel(te_ref, nu_ref, rows_ref, x_hbm, wgu_ref, bgu_ref, wdn_ref, bdn_ref,
                y_ref, xbuf, sem):
    i = pl.program_id(0)

    @pl.when(i < nu_ref[0])
    def _():
        def start(r, c):
            _row_copy(x_hbm, xbuf, sem, rows_ref[0, 0, r], r).start()
            return c

        lax.fori_loop(0, MOE_TILE, start, 0)

        def wait(r, c):
            _row_copy(x_hbm, xbuf, sem, 0, r).wait()
            return c

        lax.fori_loop(0, MOE_TILE, wait, 0)
        x = xbuf[...].astype(BF16)
        h = jnp.dot(x, wgu_ref[0], preferred_element_type=F32) + bgu_ref[0]
        gate = jnp.minimum(h[:, :D_FF], SWIGLU_LIMIT)
        up = jnp.clip(h[:, D_FF:], -SWIGLU_LIMIT, SWIGLU_LIMIT)
        glu = gate * (1.0 / (1.0 + jnp.exp(-SWIGLU_ALPHA * gate)))
        act = ((up + 1.0) * glu).astype(BF16)
        y_ref[...] = jnp.dot(act, wdn_ref[0], preferred_element_type=F32) + bdn_ref[0]

    @pl.when(i >= nu_ref[0])
    def _():
        y_ref[...] = jnp.zeros(y_ref.shape, F32)


def _moe_call(tile_e, n_used, row_tok, h1, wgu_bf, bgu, wdn_bf, bdn):
    n_tiles = tile_e.shape[0]

    def wmap(i, te, nu):
        return (te[jnp.minimum(i, nu[0] - 1)], 0, 0)

    grid_spec = pltpu.PrefetchScalarGridSpec(
        num_scalar_prefetch=2,
        grid=(n_tiles,),
        in_specs=[pl.BlockSpec((1, 1, MOE_TILE), lambda i, te, nu: (i, 0, 0), memory_space=pltpu.SMEM),
                  pl.BlockSpec(memory_space=pl.ANY),
                  pl.BlockSpec((1, D_MODEL, 2 * D_FF), wmap),
                  pl.BlockSpec((1, 1, 2 * D_FF), wmap),
                  pl.BlockSpec((1, D_FF, D_MODEL), wmap),
                  pl.BlockSpec((1, 1, D_MODEL), wmap)],
        out_specs=pl.BlockSpec((MOE_TILE, D_MODEL), lambda i, te, nu: (i, 0)),
        scratch_shapes=[pltpu.VMEM((MOE_TILE, D_MODEL), F32), pltpu.SemaphoreType.DMA(())],
    )
    return pl.pallas_call(
        _moe_kernel,
        grid_spec=grid_spec,
        out_shape=jax.ShapeDtypeStruct((n_tiles * MOE_TILE, D_MODEL), F32),
        compiler_params=_cparams(("arbitrary",)),
        name="moe_experts",
    )(tile_e, n_used, row_tok.reshape(n_tiles, 1, MOE_TILE), h1, wgu_bf,
      bgu.reshape(N_EXPERTS, 1, 2 * D_FF), wdn_bf, bdn.reshape(N_EXPERTS, 1, D_MODEL))


def _combine_kernel(pos_ref, gates_ref, h1_ref, g_ref, b_ref, ys_hbm, h2_ref, buf, sem, *, alpha):
    n_rows = TOK_TILE * TOP_K

    def start(a, c):
        _row_copy(ys_hbm, buf, sem, pos_ref[0, 0, a], a).start()
        return c

    lax.fori_loop(0, n_rows, start, 0)

    def wait(a, c):
        _row_copy(ys_hbm, buf, sem, 0, a).wait()
        return c

    lax.fori_loop(0, n_rows, wait, 0)
    gates = gates_ref[...]
    y = gates[:, 0:1] * buf[0:TOK_TILE, :]
    for j in range(1, TOP_K):
        y += gates[:, j:j + 1] * buf[j * TOK_TILE:(j + 1) * TOK_TILE, :]
    h2_ref[...] = _layer_norm(alpha * h1_ref[...] + y, g_ref[...], b_ref[...])


def _combine_call(pos, gates, h1, ln_g, ln_b, ys, alpha):
    n = h1.shape[0]
    nt = n // TOK_TILE
    row = lambda i: (i, 0)
    fix = lambda i: (0, 0)
    return pl.pallas_call(
        functools.partial(_combine_kernel, alpha=alpha),
        grid=(nt,),
        in_specs=[pl.BlockSpec((1, 1, TOK_TILE * TOP_K), lambda i: (i, 0, 0), memory_space=pltpu.SMEM),
                  pl.BlockSpec((TOK_TILE, TOP_K), row),
                  pl.BlockSpec((TOK_TILE, D_MODEL), row),
                  pl.BlockSpec((1, D_MODEL), fix),
                  pl.BlockSpec((1, D_MODEL), fix),
                  pl.BlockSpec(memory_space=pl.ANY)],
        out_specs=pl.BlockSpec((TOK_TILE, D_MODEL), row),
        out_shape=jax.ShapeDtypeStruct((n, D_MODEL), F32),
        scratch_shapes=[pltpu.VMEM((TOK_TILE * TOP_K, D_MODEL), F32), pltpu.SemaphoreType.DMA(())],
        compiler_params=_cparams(("arbitrary",)),
        name="moe_combine_ln",
    )(pos, gates, h1, ln_g, ln_b, ys)


def _route(top_idx, n_tok, n_pad):
    n_asg = n_tok * TOP_K
    flat_e = top_idx[:n_tok].reshape(-1)
    onehot = (flat_e[:, None] == jnp.arange(N_EXPERTS, dtype=jnp.int32)[None, :]).astype(jnp.int32)
    csum = jnp.cumsum(onehot, axis=0)
    rank = jnp.sum(onehot * csum, axis=1) - 1
    counts = csum[-1]
    padded = (counts + MOE_TILE - 1) // MOE_TILE * MOE_TILE
    ends = jnp.cumsum(padded)
    dest = (ends - padded)[flat_e] + rank
    n_tiles = -(-n_asg // MOE_TILE) + N_EXPERTS
    flat_tok = jnp.arange(n_asg, dtype=jnp.int32) // TOP_K
    row_tok = jnp.zeros((n_tiles * MOE_TILE,), jnp.int32).at[dest].set(flat_tok)
    tile_e = jnp.minimum(jnp.searchsorted(ends, jnp.arange(n_tiles, dtype=jnp.int32) * MOE_TILE, side="right"),
                         N_EXPERTS - 1).astype(jnp.int32)
    n_used = (ends[-1:] // MOE_TILE).astype(jnp.int32)
    pos = jnp.zeros((n_pad, TOP_K), jnp.int32).at[:n_tok].set(dest.reshape(n_tok, TOP_K).astype(jnp.int32))
    pos = pos.reshape(n_pad // TOK_TILE, TOK_TILE, TOP_K).transpose(0, 2, 1).reshape(n_pad // TOK_TILE, 1, TOK_TILE * TOP_K)
    return tile_e, n_used, row_tok, pos


def kernel(x_prompt, x_sample, cache_k_diff, cache_v_diff, cache_k_moba, cache_v_moba, page_table,
           diff_w_qkv, diff_w_o, diff_lambda_q1, diff_lambda_k1, diff_lambda_q2, diff_lambda_k2,
           diff_subln_g, moba_w_qkv, moba_w_o, ln1_g, ln1_b, ln2_g, ln2_b, moe_w_router,
           moe_b_router, moe_w_gate_up, moe_b_gate_up, moe_w_down, moe_b_down):
    batch, seq, _ = x_prompt.shape
    db = x_sample.shape[0]
    depth = ln1_g.shape[0]
    n_prompt = batch * seq
    n_tok = n_prompt + db
    n_pad = -(-n_tok // TOK_TILE) * TOK_TILE
    past_len = page_table.shape[1] * PAGE_SIZE
    alpha = (2.0 * depth) ** 0.25

    h = jnp.concatenate([x_prompt.reshape(n_prompt, D_MODEL), x_sample.reshape(db, D_MODEL),
                         jnp.zeros((n_pad - n_tok, D_MODEL), F32)], axis=0)
    pos = jnp.concatenate([jnp.tile(jnp.arange(seq, dtype=jnp.int32), batch),
                           jnp.full((n_pad - n_prompt,), past_len, jnp.int32)])
    cos, sa, sb = _rope_tables(pos)
    pad_rows = jnp.zeros((n_pad - n_tok, D_MODEL), BF16)
    sample = slice(n_prompt, n_tok)

    def flat_cache(c):
        return c.reshape(c.shape[0] * c.shape[1], PAGE_SIZE, D_MODEL)

    outs = {name: [] for name in ("kd_p", "vd_p", "km_p", "vm_p", "kd_s", "vd_s", "km_s", "vm_s")}
    for i in range(depth):
        j = i // 2
        is_diff = i % 2 == 0
        w_qkv = (diff_w_qkv if is_diff else moba_w_qkv)[j].astype(BF16)
        w_o = (diff_w_o if is_diff else moba_w_o)[j].astype(BF16)
        q_bf, k, v, k_bf, v_bf, kmean = _proj_call(h, w_qkv, cos, sa, sb)
        q_s = q_bf[sample].reshape(db, 1, D_MODEL)
        k_s = k_bf[sample].reshape(db, 1, D_MODEL)
        v_s = v_bf[sample].reshape(db, 1, D_MODEL)
        if is_diff:
            lam_init = 0.8 - 0.6 * math.exp(-0.3 * i)
            lams = [p[j].astype(F32).reshape(1, DIFF_DK)
                    for p in (diff_lambda_q1, diff_lambda_k1, diff_lambda_q2, diff_lambda_k2)]
            g = diff_subln_g[j].astype(F32).reshape(1, LANES)
            pages = page_table + j * cache_k_diff.shape[1]
            o_p = _diff_attn_call(q_bf, k_bf, v_bf, lams, g, batch, seq, lam_init)
            o_s = _diff_decode_call(pages, q_s, k_s, v_s, flat_cache(cache_k_diff), flat_cache(cache_v_diff),
                                    lams, jnp.tile(g, (1, DIFF_HEADS)), lam_init)
            pre = "d"
            kshape, vshape = (DIFF_HEADS, 2, DIFF_DK), (DIFF_HEADS, 2 * DIFF_DK)
        else:
            pages = page_table + j * cache_k_moba.shape[1]
            o_p = _moba_attn_call(q_bf, k_bf, v_bf, kmean.reshape(-1, D_MODEL), batch, seq)
            o_s = _moba_decode_call(pages, q_s, k_s, v_s, flat_cache(cache_k_moba), flat_cache(cache_v_moba))
            pre = "m"
            kshape, vshape = (MOBA_HEADS, MOBA_DH), (MOBA_HEADS, MOBA_DH)
        outs[f"k{pre}_p"].append(k[:n_prompt].reshape((batch, seq) + kshape))
        outs[f"v{pre}_p"].append(v[:n_prompt].reshape((batch, seq) + vshape))
        outs[f"k{pre}_s"].append(k[sample].reshape((db, 1) + kshape))
        outs[f"v{pre}_s"].append(v[sample].reshape((db, 1) + vshape))

        o_all = jnp.concatenate([o_p, o_s.reshape(db, D_MODEL), pad_rows], axis=0)
        wr_pad = jnp.zeros((D_MODEL, LANES), F32).at[:, :N_EXPERTS].set(moe_w_router[i].astype(F32))
        br_pad = jnp.zeros((1, LANES), F32).at[0, :N_EXPERTS].set(moe_b_router[i].astype(F32))
        h1, top_idx, gates = _outproj_call(o_all, h, w_o, ln1_g[i].reshape(1, D_MODEL),
                                           ln1_b[i].reshape(1, D_MODEL), wr_pad, br_pad, alpha)
        tile_e, n_used, row_tok, comb_pos = _route(top_idx, n_tok, n_pad)
        ys = _moe_call(tile_e, n_used, row_tok, h1, moe_w_gate_up[i].astype(BF16), moe_b_gate_up[i],
                       moe_w_down[i].astype(BF16), moe_b_down[i])
        h = _combine_call(comb_pos, gates, h1, ln2_g[i].reshape(1, D_MODEL), ln2_b[i].reshape(1, D_MODEL),
                          ys, alpha)

    y_prompt = h[:n_prompt].reshape(batch, seq, D_MODEL)
    y_sample = h[sample].reshape(db, 1, D_MODEL)
    return (y_prompt, y_sample, jnp.stack(outs["kd_p"]), jnp.stack(outs["vd_p"]),
            jnp.stack(outs["km_p"]), jnp.stack(outs["vm_p"]), jnp.stack(outs["kd_s"]),
            jnp.stack(outs["vd_s"]), jnp.stack(outs["km_s"]), jnp.stack(outs["vm_s"]))
```

```python
import functools
import math

import jax
import jax.numpy as jnp
from jax import lax
from jax.experimental import pallas as pl
from jax.experimental.pallas import tpu as pltpu

F32 = jnp.float32
BF16 = jnp.bfloat16
NEG_INF = float("-inf")

D_MODEL = 1024
PAGE_SIZE = 128
DIFF_HEADS = 8
DIFF_DK = 64
MOBA_HEADS = 16
MOBA_DH = 64
MOBA_BLOCK = 256
MOBA_TOPK = 3
N_EXPERTS = 32
TOP_K = 4
D_FF = D_MODEL
SWIGLU_LIMIT = 7.0
SWIGLU_ALPHA = 1.702
ROPE_THETA = 10000.0
LN_EPS = 1e-5
HEAD_DIM = 64
Q_SCALE = 1.0 / math.sqrt(HEAD_DIM)

LANES = 128
N_COLBLK = D_MODEL // LANES
TOK_TILE = 256
DIFF_TQ = 512
MOE_TILE = 256
DECODE_PAGES = 4
MOBA_KV_BLOCKS = 4
VMEM_LIMIT = 48 * 1024 * 1024


def _cparams(sem):
    return pltpu.CompilerParams(dimension_semantics=sem, vmem_limit_bytes=VMEM_LIMIT)


def _rope_tables(pos):
    inv = ROPE_THETA ** (-jnp.arange(0, HEAD_DIM, 2, dtype=F32) / HEAD_DIM)
    ang = pos.astype(F32)[:, None] * inv[None, :]
    ang = jnp.tile(ang, (1, LANES // (HEAD_DIM // 2)))
    cos, sin = jnp.cos(ang), jnp.sin(ang)
    first = (jnp.arange(LANES) % HEAD_DIM) < HEAD_DIM // 2
    return cos, jnp.where(first, -sin, 0.0), jnp.where(first, 0.0, sin)


def _proj_kernel(x_ref, w_ref, cos_ref, sa_ref, sb_ref,
                 q_ref, k_ref, v_ref, kb_ref, vb_ref, km_ref):
    xb = x_ref[...].astype(BF16)
    cos, sa, sb = cos_ref[...], sa_ref[...], sb_ref[...]

    def rope(a):
        half = HEAD_DIM // 2
        return a * cos + pltpu.roll(a, LANES - half, 1) * sa + pltpu.roll(a, half, 1) * sb

    aq = jnp.dot(xb, w_ref[:, 0:D_MODEL], preferred_element_type=F32)
    for c in range(N_COLBLK):
        sl = slice(c * LANES, (c + 1) * LANES)
        q_ref[:, sl] = (rope(aq[:, sl]) * Q_SCALE).astype(BF16)
    ak = jnp.dot(xb, w_ref[:, D_MODEL:2 * D_MODEL], preferred_element_type=F32)
    for c in range(N_COLBLK):
        sl = slice(c * LANES, (c + 1) * LANES)
        kr = rope(ak[:, sl])
        k_ref[:, sl] = kr
        kb_ref[:, sl] = kr.astype(BF16)
        km_ref[0, :, sl] = jnp.mean(kr, axis=0, keepdims=True)
    av = jnp.dot(xb, w_ref[:, 2 * D_MODEL:3 * D_MODEL], preferred_element_type=F32)
    v_ref[...] = av
    vb_ref[...] = av.astype(BF16)


def _proj_call(x, w_bf, cos, sa, sb):
    n = x.shape[0]
    nt = n // TOK_TILE
    row = lambda i: (i, 0)
    fix = lambda i: (0, 0)
    return pl.pallas_call(
        _proj_kernel,
        grid=(nt,),
        in_specs=[pl.BlockSpec((TOK_TILE, D_MODEL), row),
                  pl.BlockSpec((D_MODEL, 3 * D_MODEL), fix),
                  pl.BlockSpec((TOK_TILE, LANES), row),
                  pl.BlockSpec((TOK_TILE, LANES), row),
                  pl.BlockSpec((TOK_TILE, LANES), row)],
        out_specs=[pl.BlockSpec((TOK_TILE, D_MODEL), row),
                   pl.BlockSpec((TOK_TILE, D_MODEL), row),
                   pl.BlockSpec((TOK_TILE, D_MODEL), row),
                   pl.BlockSpec((TOK_TILE, D_MODEL), row),
                   pl.BlockSpec((TOK_TILE, D_MODEL), row),
                   pl.BlockSpec((1, 1, D_MODEL), lambda i: (i, 0, 0))],
        out_shape=[jax.ShapeDtypeStruct((n, D_MODEL), BF16),
                   jax.ShapeDtypeStruct((n, D_MODEL), F32),
                   jax.ShapeDtypeStruct((n, D_MODEL), F32),
                   jax.ShapeDtypeStruct((n, D_MODEL), BF16),
                   jax.ShapeDtypeStruct((n, D_MODEL), BF16),
                   jax.ShapeDtypeStruct((nt, 1, D_MODEL), F32)],
        compiler_params=_cparams(("arbitrary",)),
        name="qkv_rope",
    )(x, w_bf, cos, sa, sb)


def _half_masks(q):
    lane = lax.broadcasted_iota(jnp.int32, q.shape, 1)
    qf = q.astype(F32)
    return (jnp.where(lane < HEAD_DIM, qf, 0.0).astype(q.dtype),
            jnp.where(lane >= HEAD_DIM, qf, 0.0).astype(q.dtype))


def _dot_nt(a, b):
    return lax.dot_general(a, b, (((1,), (1,)), ((), ())), preferred_element_type=F32)


def _with_ones(vt):
    return jnp.concatenate([vt, jnp.ones(vt.shape, vt.dtype)], axis=1)


def _flash_update(c, s, v_aug, m_ref, acc_ref):
    m_prev = m_ref[c]
    m_new = jnp.maximum(m_prev, jnp.max(s, axis=1, keepdims=True))
    alpha = jnp.exp(m_prev - m_new)
    p = jnp.concatenate([jnp.exp(s[:, j * LANES:(j + 1) * LANES] - m_new)
                         for j in range(s.shape[1] // LANES)], axis=1)
    acc_ref[c] = (jnp.concatenate([alpha, alpha], axis=1) * acc_ref[c]
                  + jnp.dot(p.astype(BF16), v_aug, preferred_element_type=F32))
    m_ref[c] = m_new


def _flash_result(c, acc_ref):
    acc = acc_ref[c]
    return acc[:, :LANES] / acc[:, LANES:]


def _diff_lambda(lq1, lk1, lq2, lk2, lam_init):
    return (jnp.exp(jnp.sum(lq1 * lk1, axis=1, keepdims=True))
            - jnp.exp(jnp.sum(lq2 * lk2, axis=1, keepdims=True)) + lam_init)


def _diff_attn_kernel(q_ref, k_ref, v_ref, lq1_ref, lk1_ref, lq2_ref, lk2_ref, g_ref,
                      o_ref, m_ref, acc_ref, *, lam_init):
    t = DIFF_TQ
    qi = pl.program_id(2)
    q_maps = _half_masks(q_ref[...])
    m_ref[...] = jnp.full(m_ref.shape, NEG_INF, F32)
    acc_ref[...] = jnp.zeros(acc_ref.shape, F32)

    def tile(kj, causal):
        start = pl.multiple_of(kj * t, t)
        kt = k_ref[pl.ds(start, t), :]
        v_aug = _with_ones(v_ref[pl.ds(start, t), :])
        for c in range(2):
            s = _dot_nt(q_maps[c], kt)
            if causal:
                row = lax.broadcasted_iota(jnp.int32, s.shape, 0)
                col = lax.broadcasted_iota(jnp.int32, s.shape, 1)
                s = jnp.where(col <= row, s, NEG_INF)
            _flash_update(c, s, v_aug, m_ref, acc_ref)

    def body(kj, carry):
        tile(kj, False)
        return carry

    lax.fori_loop(0, qi, body, 0)
    tile(qi, True)

    lam = _diff_lambda(lq1_ref[...], lk1_ref[...], lq2_ref[...], lk2_ref[...], lam_init)
    o = _flash_result(0, acc_ref) - lam * _flash_result(1, acc_ref)
    ms = jnp.mean(o * o, axis=1, keepdims=True)
    o = o * lax.rsqrt(ms + LN_EPS) * g_ref[...] * (1.0 - lam_init)
    o_ref[...] = o.astype(BF16)


def _diff_attn_call(q_bf, k_bf, v_bf, lams, g, batch, seq, lam_init):
    t = DIFF_TQ
    nq = seq // t
    qmap = lambda b, h, i: (b * nq + i, h)
    kvmap = lambda b, h, i: (b, h)
    fix = lambda b, h, i: (0, 0)
    return pl.pallas_call(
        functools.partial(_diff_attn_kernel, lam_init=lam_init),
        grid=(batch, DIFF_HEADS, nq),
        in_specs=[pl.BlockSpec((t, LANES), qmap),
                  pl.BlockSpec((seq, LANES), kvmap),
                  pl.BlockSpec((seq, LANES), kvmap)]
                 + [pl.BlockSpec((1, DIFF_DK), fix)] * 4
                 + [pl.BlockSpec((1, LANES), fix)],
        out_specs=pl.BlockSpec((t, LANES), qmap),
        out_shape=jax.ShapeDtypeStruct((batch * seq, D_MODEL), BF16),
        scratch_shapes=[pltpu.VMEM((2, t, LANES), F32), pltpu.VMEM((2, t, 2 * LANES), F32)],
        compiler_params=_cparams(("arbitrary", "arbitrary", "arbitrary")),
        name="diff_attn_prompt",
    )(q_bf, k_bf, v_bf, *lams, g)


def _row_query(q, width):
    rows = D_MODEL // width
    lane = lax.broadcasted_iota(jnp.int32, (rows, D_MODEL), 1)
    row = lax.broadcasted_iota(jnp.int32, (rows, D_MODEL), 0)
    qb = jnp.broadcast_to(q.astype(F32), (rows, D_MODEL))
    return jnp.where(lane // width == row, qb, 0.0).astype(q.dtype)


def _map_major_query(q):
    rows = 2 * DIFF_HEADS
    lane = lax.broadcasted_iota(jnp.int32, (rows, D_MODEL), 1)
    row = lax.broadcasted_iota(jnp.int32, (rows, D_MODEL), 0)
    qb = jnp.broadcast_to(q.astype(F32), (rows, D_MODEL))
    chunk = 2 * (row % DIFF_HEADS) + row // DIFF_HEADS
    return jnp.where(lane // DIFF_DK == chunk, qb, 0.0).astype(q.dtype)


def _diff_decode_kernel(pt_ref, q_ref, kn_ref, vn_ref, e_ref, *rest, lam_init):
    np_ = DECODE_PAGES
    k_refs, v_refs = rest[:np_], rest[np_:2 * np_]
    lq1_ref, lk1_ref, lq2_ref, lk2_ref, g_ref, o_ref, m_ref, l_ref, acc_ref = rest[2 * np_:]
    g = pl.program_id(1)
    rows = 2 * DIFF_HEADS
    qm = _map_major_query(q_ref[0])

    @pl.when(g == 0)
    def _():
        s_self = jnp.sum(qm.astype(F32) * kn_ref[0].astype(F32), axis=1, keepdims=True)
        m_ref[...] = s_self
        l_ref[...] = jnp.ones(l_ref.shape, F32)
        vn = vn_ref[0].astype(F32)
        acc_ref[...] = jnp.concatenate([vn, vn], axis=0)

    s = jnp.concatenate([jnp.dot(qm, k_refs[i][...].astype(BF16), preferred_element_type=F32)
                         for i in range(np_)], axis=1)
    m_prev = m_ref[...]
    m_new = jnp.maximum(m_prev, jnp.max(s, axis=1, keepdims=True))
    alpha = jnp.exp(m_prev - m_new)
    p = jnp.exp(s - m_new)
    l_ref[...] = alpha * l_ref[...] + jnp.sum(p, axis=1, keepdims=True)
    p = p.astype(BF16)
    p_rows = jnp.concatenate([p[:, i * PAGE_SIZE:(i + 1) * PAGE_SIZE] for i in range(np_)], axis=0)
    spread = jnp.dot(p_rows, e_ref[...], preferred_element_type=F32)
    srow = lax.broadcasted_iota(jnp.int32, spread.shape, 0)
    slane = lax.broadcasted_iota(jnp.int32, spread.shape, 1)
    spread = jnp.where(slane % DIFF_HEADS == srow % DIFF_HEADS, spread, 0.0).astype(BF16)
    pv = None
    for i in range(np_):
        part = jnp.dot(spread[i * rows:(i + 1) * rows, :], v_refs[i][...].astype(BF16),
                       preferred_element_type=F32)
        pv = part if pv is None else pv + part
    acc_ref[...] = alpha * acc_ref[...] + pv
    m_ref[...] = m_new

    @pl.when(g == pl.num_programs(1) - 1)
    def _():
        accn = acc_ref[...] / l_ref[...]
        lam = _diff_lambda(lq1_ref[...], lk1_ref[...], lq2_ref[...], lk2_ref[...], lam_init)
        o = accn[0:DIFF_HEADS] - lam * accn[DIFF_HEADS:rows]
        ms = jnp.mean(o * o, axis=1, keepdims=True)
        o_ref[0] = (o * lax.rsqrt(ms + LN_EPS) * g_ref[...] * (1.0 - lam_init)).astype(BF16)


def _diff_decode_call(page_ids, q_s, k_new, v_new, cache_k, cache_v, lams, g, lam_init):
    db, n_pages = page_ids.shape
    np_ = DECODE_PAGES
    seq3 = lambda b, g_, pt: (b, 0, 0)
    fix = lambda b, g_, pt: (0, 0)
    tok = jnp.arange(PAGE_SIZE, dtype=jnp.int32)[:, None]
    rowid = jnp.arange(PAGE_SIZE * DIFF_HEADS, dtype=jnp.int32)[None, :]
    spread_mat = (rowid // DIFF_HEADS == tok).astype(BF16)

    def page(i):
        return pl.BlockSpec((None, D_MODEL, PAGE_SIZE), lambda b, g_, pt: (pt[b, g_ * np_ + i], 0, 0))

    grid_spec = pltpu.PrefetchScalarGridSpec(
        num_scalar_prefetch=1,
        grid=(db, n_pages // np_),
        in_specs=[pl.BlockSpec((1, 1, D_MODEL), seq3)] * 2
                 + [pl.BlockSpec((1, DIFF_HEADS, LANES), seq3)]
                 + [pl.BlockSpec((PAGE_SIZE, D_MODEL), fix)]
                 + [page(i) for i in range(np_)] + [page(i) for i in range(np_)]
                 + [pl.BlockSpec((1, DIFF_DK), fix)] * 4
                 + [pl.BlockSpec((1, LANES), fix)],
        out_specs=pl.BlockSpec((1, DIFF_HEADS, LANES), seq3),
        scratch_shapes=[pltpu.VMEM((2 * DIFF_HEADS, 1), F32), pltpu.VMEM((2 * DIFF_HEADS, 1), F32),
                        pltpu.VMEM((2 * DIFF_HEADS, LANES), F32)],
    )
    return pl.pallas_call(
        functools.partial(_diff_decode_kernel, lam_init=lam_init),
        grid_spec=grid_spec,
        out_shape=jax.ShapeDtypeStruct((db, DIFF_HEADS, LANES), BF16),
        compiler_params=_cparams(("arbitrary", "arbitrary")),
        name="diff_attn_decode",
    )(page_ids, q_s, k_new, v_new.reshape(db, DIFF_HEADS, LANES), spread_mat,
      *([cache_k] * np_), *([cache_v] * np_), *lams, g)


def _top_blocks(g, blk, n_valid):
    g = jnp.where(blk < n_valid, g, NEG_INF)
    n_blk = g.shape[1]
    sel = jnp.zeros(g.shape, F32)
    for _ in range(MOBA_TOPK):
        mx = jnp.max(g, axis=1, keepdims=True)
        ix = jnp.min(jnp.where(g == mx, blk, n_blk), axis=1, keepdims=True)
        pick = blk == ix
        sel = jnp.where(pick & (ix < n_valid), 1.0, sel)
        g = jnp.where(pick, NEG_INF, g)
    return sel


def _moba_attn_kernel(q_ref, k_ref, v_ref, km_ref, o_ref, m_ref, acc_ref, sel_ref):
    t = MOBA_BLOCK
    qi = pl.program_id(2)
    q_heads = _half_masks(q_ref[...])
    km = km_ref[...].astype(BF16)
    blk = lax.broadcasted_iota(jnp.int32, (t, km.shape[0]), 1)
    for c in range(2):
        sel_ref[c] = _top_blocks(_dot_nt(q_heads[c], km), blk, qi)
    m_ref[...] = jnp.full(m_ref.shape, NEG_INF, F32)
    acc_ref[...] = jnp.zeros(acc_ref.shape, F32)

    start = pl.multiple_of(qi * t, t)
    kt = k_ref[pl.ds(start, t), :]
    v_aug = _with_ones(v_ref[pl.ds(start, t), :])
    row = lax.broadcasted_iota(jnp.int32, (t, t), 0)
    col = lax.broadcasted_iota(jnp.int32, (t, t), 1)
    for c in range(2):
        s = jnp.where(col <= row, _dot_nt(q_heads[c], kt), NEG_INF)
        _flash_update(c, s, v_aug, m_ref, acc_ref)

    nb = MOBA_KV_BLOCKS

    def body(kj, carry):
        start = pl.multiple_of(kj * (nb * t), nb * t)
        kt = k_ref[pl.ds(start, nb * t), :]
        v_aug = _with_ones(v_ref[pl.ds(start, nb * t), :])
        for c in range(2):
            s = _dot_nt(q_heads[c], kt)
            parts = []
            for i in range(nb):
                chosen = jnp.max(jnp.where(blk == kj * nb + i, sel_ref[c], 0.0), axis=1, keepdims=True)
                parts.append(jnp.where(chosen > 0.0, s[:, i * t:(i + 1) * t], NEG_INF))
            _flash_update(c, jnp.concatenate(parts, axis=1), v_aug, m_ref, acc_ref)
        return carry

    lax.fori_loop(0, (qi + nb - 1) // nb, body, 0)
    lane = lax.broadcasted_iota(jnp.int32, (t, LANES), 1)
    o = jnp.where(lane < HEAD_DIM, _flash_result(0, acc_ref), _flash_result(1, acc_ref))
    o_ref[...] = o.astype(BF16)


def _moba_attn_call(q_bf, k_bf, v_bf, kmean, batch, seq):
    t = MOBA_BLOCK
    nq = seq // t
    qmap = lambda b, h, i: (b * nq + i, h)
    kvmap = lambda b, h, i: (b, h)
    return pl.pallas_call(
        _moba_attn_kernel,
        grid=(batch, MOBA_HEADS // 2, nq),
        in_specs=[pl.BlockSpec((t, LANES), qmap),
                  pl.BlockSpec((seq, LANES), kvmap),
                  pl.BlockSpec((seq, LANES), kvmap),
                  pl.BlockSpec((nq, LANES), kvmap)],
        out_specs=pl.BlockSpec((t, LANES), qmap),
        out_shape=jax.ShapeDtypeStruct((batch * seq, D_MODEL), BF16),
        scratch_shapes=[pltpu.VMEM((2, t, LANES), F32), pltpu.VMEM((2, t, 2 * LANES), F32),
                        pltpu.VMEM((2, t, nq), F32)],
        compiler_params=_cparams(("arbitrary", "arbitrary", "arbitrary")),
        name="moba_attn_prompt",
    )(q_bf, k_bf, v_bf, kmean)


def _moba_decode_kernel(pt_ref, q_ref, kn_ref, vn_ref, k0_ref, k1_ref, v0_ref, v1_ref,
                        o_ref, g_ref, m_ref, l_ref, part_ref):
    n = pl.program_id(1)
    n_blk = pl.num_programs(1)
    qm = _row_query(q_ref[0], MOBA_DH)
    col = lax.broadcasted_iota(jnp.int32, g_ref.shape, 1)

    @pl.when(n == 0)
    def _():
        g_ref[...] = jnp.full(g_ref.shape, NEG_INF, F32)
        m_ref[...] = jnp.zeros(m_ref.shape, F32)
        l_ref[...] = jnp.zeros(l_ref.shape, F32)

    s = jnp.concatenate([jnp.dot(qm, k0_ref[...].astype(BF16), preferred_element_type=F32),
                         jnp.dot(qm, k1_ref[...].astype(BF16), preferred_element_type=F32)], axis=1)
    gate = jnp.sum(s, axis=1, keepdims=True) * (1.0 / MOBA_BLOCK)
    m_blk = jnp.max(s, axis=1, keepdims=True)
    p = jnp.exp(s - m_blk)
    l_blk = jnp.sum(p, axis=1, keepdims=True)
    p = p.astype(BF16)
    part_ref[n] = (_dot_nt(p[:, 0:PAGE_SIZE], v0_ref[...].astype(BF16))
                   + _dot_nt(p[:, PAGE_SIZE:], v1_ref[...].astype(BF16)))
    here = col == n
    g_ref[...] = jnp.where(here, gate, g_ref[...])
    m_ref[...] = jnp.where(here, m_blk, m_ref[...])
    l_ref[...] = jnp.where(here, l_blk, l_ref[...])

    @pl.when(n == n_blk - 1)
    def _():
        sel = _top_blocks(g_ref[...], col, n_blk)
        s_self = jnp.sum(qm.astype(F32) * kn_ref[0].astype(F32), axis=1, keepdims=True)
        m_all = m_ref[...]
        m_fin = jnp.maximum(jnp.max(jnp.where(sel > 0.0, m_all, NEG_INF), axis=1, keepdims=True), s_self)
        w = jnp.where(sel > 0.0, jnp.exp(m_all - m_fin), 0.0)
        w_self = jnp.exp(s_self - m_fin)
        l_fin = jnp.sum(w * l_ref[...], axis=1, keepdims=True) + w_self
        acc = w_self * jnp.broadcast_to(vn_ref[0].astype(F32), (MOBA_HEADS, D_MODEL))

        def add(j, a):
            wj = jnp.sum(jnp.where(col == j, w, 0.0), axis=1, keepdims=True)
            return a + wj * part_ref[j]

        acc = lax.fori_loop(0, n_blk, add, acc)
        accn = acc / l_fin
        row = lax.broadcasted_iota(jnp.int32, accn.shape, 0)
        lane = lax.broadcasted_iota(jnp.int32, accn.shape, 1)
        o = jnp.sum(jnp.where(lane // MOBA_DH == row, accn, 0.0), axis=0, keepdims=True)
        o_ref[0] = o.astype(BF16)


def _moba_decode_call(page_ids, q_s, k_new, v_new, cache_k, cache_v):
    db, n_pages = page_ids.shape
    n_blk = n_pages * PAGE_SIZE // MOBA_BLOCK
    seq3 = lambda b, n, pt: (b, 0, 0)

    def page(i):
        return pl.BlockSpec((None, D_MODEL, PAGE_SIZE), lambda b, n, pt: (pt[b, 2 * n + i], 0, 0))

    grid_spec = pltpu.PrefetchScalarGridSpec(
        num_scalar_prefetch=1,
        grid=(db, n_blk),
        in_specs=[pl.BlockSpec((1, 1, D_MODEL), seq3)] * 3 + [page(0), page(1), page(0), page(1)],
        out_specs=pl.BlockSpec((1, 1, D_MODEL), seq3),
        scratch_shapes=[pltpu.VMEM((MOBA_HEADS, LANES), F32), pltpu.VMEM((MOBA_HEADS, LANES), F32),
                        pltpu.VMEM((MOBA_HEADS, LANES), F32),
                        pltpu.VMEM((n_blk, MOBA_HEADS, D_MODEL), F32)],
    )
    return pl.pallas_call(
        _moba_decode_kernel,
        grid_spec=grid_spec,
        out_shape=jax.ShapeDtypeStruct((db, 1, D_MODEL), BF16),
        compiler_params=_cparams(("arbitrary", "arbitrary")),
        name="moba_attn_decode",
    )(page_ids, q_s, k_new, v_new, cache_k, cache_k, cache_v, cache_v)


def _layer_norm(z, g, b):
    mu = jnp.mean(z, axis=1, keepdims=True)
    zc = z - mu
    var = jnp.mean(zc * zc, axis=1, keepdims=True)
    return zc * lax.rsqrt(var + LN_EPS) * g + b


def _outproj_kernel(o_ref, h_ref, wo_ref, g_ref, b_ref, wr_ref, br_ref,
                    h1_ref, idx_ref, gate_ref, *, alpha):
    mix = jnp.dot(o_ref[...], wo_ref[...], preferred_element_type=F32)
    h1 = _layer_norm(alpha * h_ref[...] + mix, g_ref[...], b_ref[...])
    h1_ref[...] = h1
    logits = jnp.dot(h1, wr_ref[...], preferred_element_type=F32,
                     precision=lax.Precision.HIGHEST) + br_ref[...]
    lane = lax.broadcasted_iota(jnp.int32, logits.shape, 1)
    logits = jnp.where(lane < N_EXPERTS, logits, NEG_INF)
    vals, idxs = [], []
    for _ in range(TOP_K):
        mx = jnp.max(logits, axis=1, keepdims=True)
        ix = jnp.min(jnp.where(logits == mx, lane, LANES), axis=1, keepdims=True)
        vals.append(mx)
        idxs.append(ix)
        logits = jnp.where(lane == ix, NEG_INF, logits)
    es = [jnp.exp(v - vals[0]) for v in vals]
    tot = es[0] + es[1] + es[2] + es[3]
    for j in range(TOP_K):
        idx_ref[:, j:j + 1] = idxs[j]
        gate_ref[:, j:j + 1] = es[j] / tot


def _outproj_call(o_bf, h, wo_bf, ln_g, ln_b, wr_pad, br_pad, alpha):
    n = h.shape[0]
    row = lambda i: (i, 0)
    fix = lambda i: (0, 0)
    return pl.pallas_call(
        functools.partial(_outproj_kernel, alpha=alpha),
        grid=(n // TOK_TILE,),
        in_specs=[pl.BlockSpec((TOK_TILE, D_MODEL), row),
                  pl.BlockSpec((TOK_TILE, D_MODEL), row),
                  pl.BlockSpec((D_MODEL, D_MODEL), fix),
                  pl.BlockSpec((1, D_MODEL), fix),
                  pl.BlockSpec((1, D_MODEL), fix),
                  pl.BlockSpec((D_MODEL, LANES), fix),
                  pl.BlockSpec((1, LANES), fix)],
        out_specs=[pl.BlockSpec((TOK_TILE, D_MODEL), row),
                   pl.BlockSpec((TOK_TILE, TOP_K), row),
                   pl.BlockSpec((TOK_TILE, TOP_K), row)],
        out_shape=[jax.ShapeDtypeStruct((n, D_MODEL), F32),
                   jax.ShapeDtypeStruct((n, TOP_K), jnp.int32),
                   jax.ShapeDtypeStruct((n, TOP_K), F32)],
        compiler_params=_cparams(("arbitrary",)),
        name="outproj_ln_router",
    )(o_bf, h, wo_bf, ln_g, ln_b, wr_pad, br_pad)


def _row_copy(src_hbm, dst_vmem, sem, src_row, dst_row):
    return pltpu.make_async_copy(src_hbm.at[pl.ds(src_row, 1)], dst_vmem.at[pl.ds(dst_row, 1)], sem)


def _moe_kernel(te_ref, nu_ref, rows_ref, x_hbm, wgu_ref, bgu_ref, wdn_ref, bdn_ref,
                y_ref, xbuf, sem):
    i = pl.program_id(0)

    @pl.when(i < nu_ref[0])
    def _():
        def start(r, c):
            _row_copy(x_hbm, xbuf, sem, rows_ref[0, 0, r], r).start()
            return c

        lax.fori_loop(0, MOE_TILE, start, 0, unroll=8)

        def wait(r, c):
            _row_copy(x_hbm, xbuf, sem, 0, r).wait()
            return c

        lax.fori_loop(0, MOE_TILE, wait, 0, unroll=8)
        x = xbuf[...].astype(BF16)
        h = jnp.dot(x, wgu_ref[0], preferred_element_type=F32) + bgu_ref[0]
        gate = jnp.minimum(h[:, :D_FF], SWIGLU_LIMIT)
        up = jnp.clip(h[:, D_FF:], -SWIGLU_LIMIT, SWIGLU_LIMIT)
        glu = gate * (1.0 / (1.0 + jnp.exp(-SWIGLU_ALPHA * gate)))
        act = ((up + 1.0) * glu).astype(BF16)
        y_ref[...] = jnp.dot(act, wdn_ref[0], preferred_element_type=F32) + bdn_ref[0]

    @pl.when(i >= nu_ref[0])
    def _():
        y_ref[...] = jnp.zeros(y_ref.shape, F32)


def _moe_call(tile_e, n_used, row_tok, h1, wgu_bf, bgu, wdn_bf, bdn):
    n_tiles = tile_e.shape[0]

    def wmap(i, te, nu):
        return (te[jnp.minimum(i, nu[0] - 1)], 0, 0)

    grid_spec = pltpu.PrefetchScalarGridSpec(
        num_scalar_prefetch=2,
        grid=(n_tiles,),
        in_specs=[pl.BlockSpec((1, 1, MOE_TILE), lambda i, te, nu: (i, 0, 0), memory_space=pltpu.SMEM),
                  pl.BlockSpec(memory_space=pl.ANY),
                  pl.BlockSpec((1, D_MODEL, 2 * D_FF), wmap),
                  pl.BlockSpec((1, 1, 2 * D_FF), wmap),
                  pl.BlockSpec((1, D_FF, D_MODEL), wmap),
                  pl.BlockSpec((1, 1, D_MODEL), wmap)],
        out_specs=pl.BlockSpec((MOE_TILE, D_MODEL), lambda i, te, nu: (i, 0)),
        scratch_shapes=[pltpu.VMEM((MOE_TILE, D_MODEL), F32), pltpu.SemaphoreType.DMA(())],
    )
    return pl.pallas_call(
        _moe_kernel,
        grid_spec=grid_spec,
        out_shape=jax.ShapeDtypeStruct((n_tiles * MOE_TILE, D_MODEL), F32),
        compiler_params=_cparams(("arbitrary",)),
        name="moe_experts",
    )(tile_e, n_used, row_tok.reshape(n_tiles, 1, MOE_TILE), h1, wgu_bf,
      bgu.reshape(N_EXPERTS, 1, 2 * D_FF), wdn_bf, bdn.reshape(N_EXPERTS, 1, D_MODEL))


def _combine_kernel(pos_ref, gates_ref, h1_ref, g_ref, b_ref, ys_hbm, h2_ref, buf, sem, *, alpha):
    n_rows = TOK_TILE * TOP_K

    def start(a, c):
        _row_copy(ys_hbm, buf, sem, pos_ref[0, 0, a], a).start()
        return c

    lax.fori_loop(0, n_rows, start, 0, unroll=8)

    def wait(a, c):
        _row_copy(ys_hbm, buf, sem, 0, a).wait()
        return c

    lax.fori_loop(0, n_rows, wait, 0, unroll=8)
    gates = gates_ref[...]
    y = gates[:, 0:1] * buf[0:TOK_TILE, :]
    for j in range(1, TOP_K):
        y += gates[:, j:j + 1] * buf[j * TOK_TILE:(j + 1) * TOK_TILE, :]
    h2_ref[...] = _layer_norm(alpha * h1_ref[...] + y, g_ref[...], b_ref[...])


def _combine_call(pos, gates, h1, ln_g, ln_b, ys, alpha):
    n = h1.shape[0]
    nt = n // TOK_TILE
    row = lambda i: (i, 0)
    fix = lambda i: (0, 0)
    return pl.pallas_call(
        functools.partial(_combine_kernel, alpha=alpha),
        grid=(nt,),
        in_specs=[pl.BlockSpec((1, 1, TOK_TILE * TOP_K), lambda i: (i, 0, 0), memory_space=pltpu.SMEM),
                  pl.BlockSpec((TOK_TILE, TOP_K), row),
                  pl.BlockSpec((TOK_TILE, D_MODEL), row),
                  pl.BlockSpec((1, D_MODEL), fix),
                  pl.BlockSpec((1, D_MODEL), fix),
                  pl.BlockSpec(memory_space=pl.ANY)],
        out_specs=pl.BlockSpec((TOK_TILE, D_MODEL), row),
        out_shape=jax.ShapeDtypeStruct((n, D_MODEL), F32),
        scratch_shapes=[pltpu.VMEM((TOK_TILE * TOP_K, D_MODEL), F32), pltpu.SemaphoreType.DMA(())],
        compiler_params=_cparams(("arbitrary",)),
        name="moe_combine_ln",
    )(pos, gates, h1, ln_g, ln_b, ys)


def _route(top_idx, n_tok, n_pad):
    n_asg = n_tok * TOP_K
    flat_e = top_idx[:n_tok].reshape(-1)
    onehot = (flat_e[:, None] == jnp.arange(N_EXPERTS, dtype=jnp.int32)[None, :]).astype(jnp.int32)
    csum = jnp.cumsum(onehot, axis=0)
    rank = jnp.sum(onehot * csum, axis=1) - 1
    counts = csum[-1]
    padded = (counts + MOE_TILE - 1) // MOE_TILE * MOE_TILE
    ends = jnp.cumsum(padded)
    dest = (ends - padded)[flat_e] + rank
    n_tiles = -(-n_asg // MOE_TILE) + N_EXPERTS
    flat_tok = jnp.arange(n_asg, dtype=jnp.int32) // TOP_K
    row_tok = jnp.zeros((n_tiles * MOE_TILE,), jnp.int32).at[dest].set(
        flat_tok, unique_indices=True, mode="promise_in_bounds")
    tile_e = jnp.minimum(jnp.searchsorted(ends, jnp.arange(n_tiles, dtype=jnp.int32) * MOE_TILE, side="right"),
                         N_EXPERTS - 1).astype(jnp.int32)
    n_used = (ends[-1:] // MOE_TILE).astype(jnp.int32)
    pos = jnp.zeros((n_pad, TOP_K), jnp.int32).at[:n_tok].set(dest.reshape(n_tok, TOP_K).astype(jnp.int32))
    pos = pos.reshape(n_pad // TOK_TILE, TOK_TILE, TOP_K).transpose(0, 2, 1).reshape(n_pad // TOK_TILE, 1, TOK_TILE * TOP_K)
    return tile_e, n_used, row_tok, pos


def kernel(x_prompt, x_sample, cache_k_diff, cache_v_diff, cache_k_moba, cache_v_moba, page_table,
           diff_w_qkv, diff_w_o, diff_lambda_q1, diff_lambda_k1, diff_lambda_q2, diff_lambda_k2,
           diff_subln_g, moba_w_qkv, moba_w_o, ln1_g, ln1_b, ln2_g, ln2_b, moe_w_router,
           moe_b_router, moe_w_gate_up, moe_b_gate_up, moe_w_down, moe_b_down):
    batch, seq, _ = x_prompt.shape
    db = x_sample.shape[0]
    depth = ln1_g.shape[0]
    n_prompt = batch * seq
    n_tok = n_prompt + db
    n_pad = -(-n_tok // TOK_TILE) * TOK_TILE
    past_len = page_table.shape[1] * PAGE_SIZE
    alpha = (2.0 * depth) ** 0.25

    h = jnp.concatenate([x_prompt.reshape(n_prompt, D_MODEL), x_sample.reshape(db, D_MODEL),
                         jnp.zeros((n_pad - n_tok, D_MODEL), F32)], axis=0)
    pos = jnp.concatenate([jnp.tile(jnp.arange(seq, dtype=jnp.int32), batch),
                           jnp.full((n_pad - n_prompt,), past_len, jnp.int32)])
    cos, sa, sb = _rope_tables(pos)
    pad_rows = jnp.zeros((n_pad - n_tok, D_MODEL), BF16)
    sample = slice(n_prompt, n_tok)

    def feature_major(c):
        perm = (0, 1) + tuple(range(3, c.ndim)) + (2,)
        return jnp.transpose(c, perm).reshape(c.shape[0] * c.shape[1], D_MODEL, PAGE_SIZE)

    outs = {name: [] for name in ("kd_p", "vd_p", "km_p", "vm_p", "kd_s", "vd_s", "km_s", "vm_s")}
    for i in range(depth):
        j = i // 2
        is_diff = i % 2 == 0
        w_qkv = (diff_w_qkv if is_diff else moba_w_qkv)[j].astype(BF16)
        w_o = (diff_w_o if is_diff else moba_w_o)[j].astype(BF16)
        q_bf, k, v, k_bf, v_bf, kmean = _proj_call(h, w_qkv, cos, sa, sb)
        q_s = q_bf[sample].reshape(db, 1, D_MODEL)
        k_s = k_bf[sample].reshape(db, 1, D_MODEL)
        v_s = v_bf[sample].reshape(db, 1, D_MODEL)
        if is_diff:
            lam_init = 0.8 - 0.6 * math.exp(-0.3 * i)
            lams = [p[j].astype(F32).reshape(1, DIFF_DK)
                    for p in (diff_lambda_q1, diff_lambda_k1, diff_lambda_q2, diff_lambda_k2)]
            g = diff_subln_g[j].astype(F32).reshape(1, LANES)
            pages = page_table + j * cache_k_diff.shape[1]
            o_p = _diff_attn_call(q_bf, k_bf, v_bf, lams, g, batch, seq, lam_init)
            v_pages = cache_v_diff.reshape(-1, PAGE_SIZE * DIFF_HEADS, LANES)
            o_s = _diff_decode_call(pages, q_s, k_s, v_s, feature_major(cache_k_diff), v_pages,
                                    lams, g, lam_init)
            pre = "d"
            kshape, vshape = (DIFF_HEADS, 2, DIFF_DK), (DIFF_HEADS, 2 * DIFF_DK)
        else:
            pages = page_table + j * cache_k_moba.shape[1]
            o_p = _moba_attn_call(q_bf, k_bf, v_bf, kmean.reshape(-1, D_MODEL), batch, seq)
            o_s = _moba_decode_call(pages, q_s, k_s, v_s, feature_major(cache_k_moba),
                                    feature_major(cache_v_moba))
            pre = "m"
            kshape, vshape = (MOBA_HEADS, MOBA_DH), (MOBA_HEADS, MOBA_DH)
        outs[f"k{pre}_p"].append(k[:n_prompt].reshape((batch, seq) + kshape))
        outs[f"v{pre}_p"].append(v[:n_prompt].reshape((batch, seq) + vshape))
        outs[f"k{pre}_s"].append(k[sample].reshape((db, 1) + kshape))
        outs[f"v{pre}_s"].append(v[sample].reshape((db, 1) + vshape))

        o_all = jnp.concatenate([o_p, o_s.reshape(db, D_MODEL), pad_rows], axis=0)
        wr_pad = jnp.zeros((D_MODEL, LANES), F32).at[:, :N_EXPERTS].set(moe_w_router[i].astype(F32))
        br_pad = jnp.zeros((1, LANES), F32).at[0, :N_EXPERTS].set(moe_b_router[i].astype(F32))
        h1, top_idx, gates = _outproj_call(o_all, h, w_o, ln1_g[i].reshape(1, D_MODEL),
                                           ln1_b[i].reshape(1, D_MODEL), wr_pad, br_pad, alpha)
        tile_e, n_used, row_tok, comb_pos = _route(top_idx, n_tok, n_pad)
        ys = _moe_call(tile_e, n_used, row_tok, h1, moe_w_gate_up[i].astype(BF16), moe_b_gate_up[i],
                       moe_w_down[i].astype(BF16), moe_b_down[i])
        h = _combine_call(comb_pos, gates, h1, ln2_g[i].reshape(1, D_MODEL), ln2_b[i].reshape(1, D_MODEL),
                          ys, alpha)

    y_prompt = h[:n_prompt].reshape(batch, seq, D_MODEL)
    y_sample = h[sample].reshape(db, 1, D_MODEL)
    return (y_prompt, y_sample, jnp.stack(outs["kd_p"]), jnp.stack(outs["vd_p"]),
            jnp.stack(outs["km_p"]), jnp.stack(outs["vm_p"]), jnp.stack(outs["kd_s"]),
            jnp.stack(outs["vd_s"]), jnp.stack(outs["km_s"]), jnp.stack(outs["vm_s"]))
```

```python
import functools
import math

import jax
import jax.numpy as jnp
from jax import lax
from jax.experimental import pallas as pl
from jax.experimental.pallas import tpu as pltpu

F32 = jnp.float32
BF16 = jnp.bfloat16
NEG_INF = float("-inf")

D_MODEL = 1024
PAGE_SIZE = 128
DIFF_HEADS = 8
DIFF_DK = 64
MOBA_HEADS = 16
MOBA_DH = 64
MOBA_BLOCK = 256
MOBA_TOPK = 3
N_EXPERTS = 32
TOP_K = 4
D_FF = D_MODEL
SWIGLU_LIMIT = 7.0
SWIGLU_ALPHA = 1.702
ROPE_THETA = 10000.0
LN_EPS = 1e-5
HEAD_DIM = 64
Q_SCALE = 1.0 / math.sqrt(HEAD_DIM)

LANES = 128
N_COLBLK = D_MODEL // LANES
TOK_TILE = 256
DIFF_TQ = 512
MOE_TILE = 256
DECODE_PAGES = 4
MOBA_KV_BLOCKS = 4
VMEM_LIMIT = 48 * 1024 * 1024
MOE_VMEM_LIMIT = 56 * 1024 * 1024


def _cparams(sem, vmem=VMEM_LIMIT):
    return pltpu.CompilerParams(dimension_semantics=sem, vmem_limit_bytes=vmem)


def _rope_tables(pos):
    inv = ROPE_THETA ** (-jnp.arange(0, HEAD_DIM, 2, dtype=F32) / HEAD_DIM)
    ang = pos.astype(F32)[:, None] * inv[None, :]
    ang = jnp.tile(ang, (1, LANES // (HEAD_DIM // 2)))
    cos, sin = jnp.cos(ang), jnp.sin(ang)
    first = (jnp.arange(LANES) % HEAD_DIM) < HEAD_DIM // 2
    return cos, jnp.where(first, -sin, 0.0), jnp.where(first, 0.0, sin)


def _proj_kernel(x_ref, w_ref, cos_ref, sa_ref, sb_ref,
                 q_ref, k_ref, v_ref, kb_ref, vb_ref, km_ref):
    xb = x_ref[...].astype(BF16)
    cos, sa, sb = cos_ref[...], sa_ref[...], sb_ref[...]

    def rope(a):
        half = HEAD_DIM // 2
        return a * cos + pltpu.roll(a, LANES - half, 1) * sa + pltpu.roll(a, half, 1) * sb

    aq = jnp.dot(xb, w_ref[:, 0:D_MODEL], preferred_element_type=F32)
    for c in range(N_COLBLK):
        sl = slice(c * LANES, (c + 1) * LANES)
        q_ref[:, sl] = (rope(aq[:, sl]) * Q_SCALE).astype(BF16)
    ak = jnp.dot(xb, w_ref[:, D_MODEL:2 * D_MODEL], preferred_element_type=F32)
    for c in range(N_COLBLK):
        sl = slice(c * LANES, (c + 1) * LANES)
        kr = rope(ak[:, sl])
        k_ref[:, sl] = kr
        kb_ref[:, sl] = kr.astype(BF16)
        km_ref[0, :, sl] = jnp.mean(kr, axis=0, keepdims=True)
    av = jnp.dot(xb, w_ref[:, 2 * D_MODEL:3 * D_MODEL], preferred_element_type=F32)
    v_ref[...] = av
    vb_ref[...] = av.astype(BF16)


def _proj_call(x, w_bf, cos, sa, sb):
    n = x.shape[0]
    nt = n // TOK_TILE
    row = lambda i: (i, 0)
    fix = lambda i: (0, 0)
    return pl.pallas_call(
        _proj_kernel,
        grid=(nt,),
        in_specs=[pl.BlockSpec((TOK_TILE, D_MODEL), row),
                  pl.BlockSpec((D_MODEL, 3 * D_MODEL), fix),
                  pl.BlockSpec((TOK_TILE, LANES), row),
                  pl.BlockSpec((TOK_TILE, LANES), row),
                  pl.BlockSpec((TOK_TILE, LANES), row)],
        out_specs=[pl.BlockSpec((TOK_TILE, D_MODEL), row),
                   pl.BlockSpec((TOK_TILE, D_MODEL), row),
                   pl.BlockSpec((TOK_TILE, D_MODEL), row),
                   pl.BlockSpec((TOK_TILE, D_MODEL), row),
                   pl.BlockSpec((TOK_TILE, D_MODEL), row),
                   pl.BlockSpec((1, 1, D_MODEL), lambda i: (i, 0, 0))],
        out_shape=[jax.ShapeDtypeStruct((n, D_MODEL), BF16),
                   jax.ShapeDtypeStruct((n, D_MODEL), F32),
                   jax.ShapeDtypeStruct((n, D_MODEL), F32),
                   jax.ShapeDtypeStruct((n, D_MODEL), BF16),
                   jax.ShapeDtypeStruct((n, D_MODEL), BF16),
                   jax.ShapeDtypeStruct((nt, 1, D_MODEL), F32)],
        compiler_params=_cparams(("arbitrary",)),
        name="qkv_rope",
    )(x, w_bf, cos, sa, sb)


def _half_masks(q):
    lane = lax.broadcasted_iota(jnp.int32, q.shape, 1)
    qf = q.astype(F32)
    return (jnp.where(lane < HEAD_DIM, qf, 0.0).astype(q.dtype),
            jnp.where(lane >= HEAD_DIM, qf, 0.0).astype(q.dtype))


def _dot_nt(a, b):
    return lax.dot_general(a, b, (((1,), (1,)), ((), ())), preferred_element_type=F32)


def _with_ones(vt):
    return jnp.concatenate([vt, jnp.ones(vt.shape, vt.dtype)], axis=1)


def _flash_update(c, s, v_aug, m_ref, acc_ref):
    m_prev = m_ref[c]
    m_new = jnp.maximum(m_prev, jnp.max(s, axis=1, keepdims=True))
    alpha = jnp.exp(m_prev - m_new)
    p = jnp.concatenate([jnp.exp(s[:, j * LANES:(j + 1) * LANES] - m_new)
                         for j in range(s.shape[1] // LANES)], axis=1)
    acc_ref[c] = (jnp.concatenate([alpha, alpha], axis=1) * acc_ref[c]
                  + jnp.dot(p.astype(BF16), v_aug, preferred_element_type=F32))
    m_ref[c] = m_new


def _flash_result(c, acc_ref):
    acc = acc_ref[c]
    return acc[:, :LANES] / acc[:, LANES:]


def _diff_lambda(lq1, lk1, lq2, lk2, lam_init):
    return (jnp.exp(jnp.sum(lq1 * lk1, axis=1, keepdims=True))
            - jnp.exp(jnp.sum(lq2 * lk2, axis=1, keepdims=True)) + lam_init)


def _diff_attn_kernel(q_ref, k_ref, v_ref, lq1_ref, lk1_ref, lq2_ref, lk2_ref, g_ref,
                      o_ref, m_ref, acc_ref, sa_ref, sb_ref, *, lam_init):
    t = DIFF_TQ
    qi = pl.program_id(2)
    q_maps = _half_masks(q_ref[...])
    m_ref[...] = jnp.full(m_ref.shape, NEG_INF, F32)
    acc_ref[...] = jnp.zeros(acc_ref.shape, F32)

    def scores(kj, s_ref):
        kt = k_ref[pl.ds(pl.multiple_of(kj * t, t), t), :]
        for c in range(2):
            s_ref[c] = _dot_nt(q_maps[c], kt)

    def consume(kj, s_ref, causal):
        v_aug = _with_ones(v_ref[pl.ds(pl.multiple_of(kj * t, t), t), :])
        for c in range(2):
            s = s_ref[c]
            if causal:
                row = lax.broadcasted_iota(jnp.int32, s.shape, 0)
                col = lax.broadcasted_iota(jnp.int32, s.shape, 1)
                s = jnp.where(col <= row, s, NEG_INF)
            _flash_update(c, s, v_aug, m_ref, acc_ref)

    scores(0, sa_ref)

    def pair(mi, carry):
        kj = 2 * mi
        scores(kj + 1, sb_ref)
        consume(kj, sa_ref, False)
        scores(kj + 2, sa_ref)
        consume(kj + 1, sb_ref, False)
        return carry

    lax.fori_loop(0, qi // 2, pair, 0)

    @pl.when(qi % 2 == 0)
    def _():
        consume(qi, sa_ref, True)

    @pl.when(qi % 2 == 1)
    def _():
        scores(qi, sb_ref)
        consume(qi - 1, sa_ref, False)
        consume(qi, sb_ref, True)

    lam = _diff_lambda(lq1_ref[...], lk1_ref[...], lq2_ref[...], lk2_ref[...], lam_init)
    o = _flash_result(0, acc_ref) - lam * _flash_result(1, acc_ref)
    ms = jnp.mean(o * o, axis=1, keepdims=True)
    o = o * lax.rsqrt(ms + LN_EPS) * g_ref[...] * (1.0 - lam_init)
    o_ref[...] = o.astype(BF16)


def _diff_attn_call(q_bf, k_bf, v_bf, lams, g, batch, seq, lam_init):
    t = DIFF_TQ
    nq = seq // t
    qmap = lambda b, h, i: (b * nq + i, h)
    kvmap = lambda b, h, i: (b, h)
    fix = lambda b, h, i: (0, 0)
    return pl.pallas_call(
        functools.partial(_diff_attn_kernel, lam_init=lam_init),
        grid=(batch, DIFF_HEADS, nq),
        in_specs=[pl.BlockSpec((t, LANES), qmap),
                  pl.BlockSpec((seq, LANES), kvmap),
                  pl.BlockSpec((seq, LANES), kvmap)]
                 + [pl.BlockSpec((1, DIFF_DK), fix)] * 4
                 + [pl.BlockSpec((1, LANES), fix)],
        out_specs=pl.BlockSpec((t, LANES), qmap),
        out_shape=jax.ShapeDtypeStruct((batch * seq, D_MODEL), BF16),
        scratch_shapes=[pltpu.VMEM((2, t, LANES), F32), pltpu.VMEM((2, t, 2 * LANES), F32),
                        pltpu.VMEM((2, t, t), F32), pltpu.VMEM((2, t, t), F32)],
        compiler_params=_cparams(("arbitrary", "arbitrary", "arbitrary")),
        name="diff_attn_prompt",
    )(q_bf, k_bf, v_bf, *lams, g)


def _row_query(q, width):
    rows = D_MODEL // width
    lane = lax.broadcasted_iota(jnp.int32, (rows, D_MODEL), 1)
    row = lax.broadcasted_iota(jnp.int32, (rows, D_MODEL), 0)
    qb = jnp.broadcast_to(q.astype(F32), (rows, D_MODEL))
    return jnp.where(lane // width == row, qb, 0.0).astype(q.dtype)


def _map_major_query(q):
    rows = 2 * DIFF_HEADS
    lane = lax.broadcasted_iota(jnp.int32, (rows, D_MODEL), 1)
    row = lax.broadcasted_iota(jnp.int32, (rows, D_MODEL), 0)
    qb = jnp.broadcast_to(q.astype(F32), (rows, D_MODEL))
    chunk = 2 * (row % DIFF_HEADS) + row // DIFF_HEADS
    return jnp.where(lane // DIFF_DK == chunk, qb, 0.0).astype(q.dtype)


def _diff_decode_kernel(pt_ref, q_ref, kn_ref, vn_ref, e_ref, *rest, lam_init):
    np_ = DECODE_PAGES
    k_refs, v_refs = rest[:np_], rest[np_:2 * np_]
    lq1_ref, lk1_ref, lq2_ref, lk2_ref, g_ref, o_ref, m_ref, l_ref, acc_ref = rest[2 * np_:]
    g = pl.program_id(1)
    rows = 2 * DIFF_HEADS
    qm = _map_major_query(q_ref[0])

    @pl.when(g == 0)
    def _():
        s_self = jnp.sum(qm.astype(F32) * kn_ref[0].astype(F32), axis=1, keepdims=True)
        m_ref[...] = s_self
        l_ref[...] = jnp.ones(l_ref.shape, F32)
        vn = vn_ref[0].astype(F32)
        acc_ref[...] = jnp.concatenate([vn, vn], axis=0)

    s = jnp.concatenate([jnp.dot(qm, k_refs[i][...].astype(BF16), preferred_element_type=F32)
                         for i in range(np_)], axis=1)
    m_prev = m_ref[...]
    m_new = jnp.maximum(m_prev, jnp.max(s, axis=1, keepdims=True))
    alpha = jnp.exp(m_prev - m_new)
    p = jnp.exp(s - m_new)
    l_ref[...] = alpha * l_ref[...] + jnp.sum(p, axis=1, keepdims=True)
    p = p.astype(BF16)
    p_rows = jnp.concatenate([p[:, i * PAGE_SIZE:(i + 1) * PAGE_SIZE] for i in range(np_)], axis=0)
    spread = jnp.dot(p_rows, e_ref[...], preferred_element_type=F32)
    srow = lax.broadcasted_iota(jnp.int32, spread.shape, 0)
    slane = lax.broadcasted_iota(jnp.int32, spread.shape, 1)
    spread = jnp.where(slane % DIFF_HEADS == srow % DIFF_HEADS, spread, 0.0).astype(BF16)
    pv = None
    for i in range(np_):
        part = jnp.dot(spread[i * rows:(i + 1) * rows, :], v_refs[i][...].astype(BF16),
                       preferred_element_type=F32)
        pv = part if pv is None else pv + part
    acc_ref[...] = alpha * acc_ref[...] + pv
    m_ref[...] = m_new

    @pl.when(g == pl.num_programs(1) - 1)
    def _():
        accn = acc_ref[...] / l_ref[...]
        lam = _diff_lambda(lq1_ref[...], lk1_ref[...], lq2_ref[...], lk2_ref[...], lam_init)
        o = accn[0:DIFF_HEADS] - lam * accn[DIFF_HEADS:rows]
        ms = jnp.mean(o * o, axis=1, keepdims=True)
        o_ref[0] = (o * lax.rsqrt(ms + LN_EPS) * g_ref[...] * (1.0 - lam_init)).astype(BF16)


def _diff_decode_call(page_ids, q_s, k_new, v_new, cache_k, cache_v, lams, g, lam_init):
    db, n_pages = page_ids.shape
    np_ = DECODE_PAGES
    seq3 = lambda b, g_, pt: (b, 0, 0)
    fix = lambda b, g_, pt: (0, 0)
    tok = jnp.arange(PAGE_SIZE, dtype=jnp.int32)[:, None]
    rowid = jnp.arange(PAGE_SIZE * DIFF_HEADS, dtype=jnp.int32)[None, :]
    spread_mat = (rowid // DIFF_HEADS == tok).astype(BF16)

    def page(i):
        return pl.BlockSpec((None, D_MODEL, PAGE_SIZE), lambda b, g_, pt: (pt[b, g_ * np_ + i], 0, 0))

    grid_spec = pltpu.PrefetchScalarGridSpec(
        num_scalar_prefetch=1,
        grid=(db, n_pages // np_),
        in_specs=[pl.BlockSpec((1, 1, D_MODEL), seq3)] * 2
                 + [pl.BlockSpec((1, DIFF_HEADS, LANES), seq3)]
                 + [pl.BlockSpec((PAGE_SIZE, D_MODEL), fix)]
                 + [page(i) for i in range(np_)] + [page(i) for i in range(np_)]
                 + [pl.BlockSpec((1, DIFF_DK), fix)] * 4
                 + [pl.BlockSpec((1, LANES), fix)],
        out_specs=pl.BlockSpec((1, DIFF_HEADS, LANES), seq3),
        scratch_shapes=[pltpu.VMEM((2 * DIFF_HEADS, 1), F32), pltpu.VMEM((2 * DIFF_HEADS, 1), F32),
                        pltpu.VMEM((2 * DIFF_HEADS, LANES), F32)],
    )
    return pl.pallas_call(
        functools.partial(_diff_decode_kernel, lam_init=lam_init),
        grid_spec=grid_spec,
        out_shape=jax.ShapeDtypeStruct((db, DIFF_HEADS, LANES), BF16),
        compiler_params=_cparams(("arbitrary", "arbitrary")),
        name="diff_attn_decode",
    )(page_ids, q_s, k_new, v_new.reshape(db, DIFF_HEADS, LANES), spread_mat,
      *([cache_k] * np_), *([cache_v] * np_), *lams, g)


def _top_blocks(g, blk, n_valid):
    g = jnp.where(blk < n_valid, g, NEG_INF)
    n_blk = g.shape[1]
    sel = jnp.zeros(g.shape, F32)
    for _ in range(MOBA_TOPK):
        mx = jnp.max(g, axis=1, keepdims=True)
        ix = jnp.min(jnp.where(g == mx, blk, n_blk), axis=1, keepdims=True)
        pick = blk == ix
        sel = jnp.where(pick & (ix < n_valid), 1.0, sel)
        g = jnp.where(pick, NEG_INF, g)
    return sel


def _moba_attn_kernel(q_ref, k_ref, v_ref, km_ref, o_ref, m_ref, acc_ref, sel_ref, sa_ref, sb_ref):
    t = MOBA_BLOCK
    nb = MOBA_KV_BLOCKS
    w = nb * t
    qi = pl.program_id(2)
    n_chunks = (qi + nb - 1) // nb
    last_chunk = k_ref.shape[0] // w - 1
    q_heads = _half_masks(q_ref[...])
    km = km_ref[...].astype(BF16)
    blk = lax.broadcasted_iota(jnp.int32, (t, km.shape[0]), 1)

    def scores(kj, s_ref):
        kj = jnp.minimum(kj, last_chunk)
        kt = k_ref[pl.ds(pl.multiple_of(kj * w, w), w), :]
        for c in range(2):
            s_ref[c] = _dot_nt(q_heads[c], kt)

    def consume(kj, s_ref):
        v_aug = _with_ones(v_ref[pl.ds(pl.multiple_of(kj * w, w), w), :])
        for c in range(2):
            s = s_ref[c]
            parts = []
            for i in range(nb):
                chosen = jnp.max(jnp.where(blk == kj * nb + i, sel_ref[c], 0.0), axis=1, keepdims=True)
                parts.append(jnp.where(chosen > 0.0, s[:, i * t:(i + 1) * t], NEG_INF))
            _flash_update(c, jnp.concatenate(parts, axis=1), v_aug, m_ref, acc_ref)

    scores(0, sa_ref)
    for c in range(2):
        sel_ref[c] = _top_blocks(_dot_nt(q_heads[c], km), blk, qi)
    m_ref[...] = jnp.full(m_ref.shape, NEG_INF, F32)
    acc_ref[...] = jnp.zeros(acc_ref.shape, F32)

    start = pl.multiple_of(qi * t, t)
    kt = k_ref[pl.ds(start, t), :]
    v_aug = _with_ones(v_ref[pl.ds(start, t), :])
    row = lax.broadcasted_iota(jnp.int32, (t, t), 0)
    col = lax.broadcasted_iota(jnp.int32, (t, t), 1)
    for c in range(2):
        s = jnp.where(col <= row, _dot_nt(q_heads[c], kt), NEG_INF)
        _flash_update(c, s, v_aug, m_ref, acc_ref)

    def pair(mi, carry):
        kj = 2 * mi
        scores(kj + 1, sb_ref)
        consume(kj, sa_ref)
        scores(kj + 2, sa_ref)
        consume(kj + 1, sb_ref)
        return carry

    lax.fori_loop(0, n_chunks // 2, pair, 0)

    @pl.when(n_chunks % 2 == 1)
    def _():
        consume(n_chunks - 1, sa_ref)

    lane = lax.broadcasted_iota(jnp.int32, (t, LANES), 1)
    o = jnp.where(lane < HEAD_DIM, _flash_result(0, acc_ref), _flash_result(1, acc_ref))
    o_ref[...] = o.astype(BF16)


def _moba_attn_call(q_bf, k_bf, v_bf, kmean, batch, seq):
    t = MOBA_BLOCK
    nq = seq // t
    qmap = lambda b, h, i: (b * nq + i, h)
    kvmap = lambda b, h, i: (b, h)
    return pl.pallas_call(
        _moba_attn_kernel,
        grid=(batch, MOBA_HEADS // 2, nq),
        in_specs=[pl.BlockSpec((t, LANES), qmap),
                  pl.BlockSpec((seq, LANES), kvmap),
                  pl.BlockSpec((seq, LANES), kvmap),
                  pl.BlockSpec((nq, LANES), kvmap)],
        out_specs=pl.BlockSpec((t, LANES), qmap),
        out_shape=jax.ShapeDtypeStruct((batch * seq, D_MODEL), BF16),
        scratch_shapes=[pltpu.VMEM((2, t, LANES), F32), pltpu.VMEM((2, t, 2 * LANES), F32),
                        pltpu.VMEM((2, t, nq), F32),
                        pltpu.VMEM((2, t, MOBA_KV_BLOCKS * t), F32),
                        pltpu.VMEM((2, t, MOBA_KV_BLOCKS * t), F32)],
        compiler_params=_cparams(("arbitrary", "arbitrary", "arbitrary")),
        name="moba_attn_prompt",
    )(q_bf, k_bf, v_bf, kmean)


def _moba_decode_kernel(pt_ref, q_ref, kn_ref, vn_ref, k0_ref, k1_ref, v0_ref, v1_ref,
                        o_ref, g_ref, m_ref, l_ref, part_ref):
    n = pl.program_id(1)
    n_blk = pl.num_programs(1)
    qm = _row_query(q_ref[0], MOBA_DH)
    col = lax.broadcasted_iota(jnp.int32, g_ref.shape, 1)

    @pl.when(n == 0)
    def _():
        g_ref[...] = jnp.full(g_ref.shape, NEG_INF, F32)
        m_ref[...] = jnp.zeros(m_ref.shape, F32)
        l_ref[...] = jnp.zeros(l_ref.shape, F32)

    s = jnp.concatenate([jnp.dot(qm, k0_ref[...].astype(BF16), preferred_element_type=F32),
                         jnp.dot(qm, k1_ref[...].astype(BF16), preferred_element_type=F32)], axis=1)
    gate = jnp.sum(s, axis=1, keepdims=True) * (1.0 / MOBA_BLOCK)
    m_blk = jnp.max(s, axis=1, keepdims=True)
    p = jnp.exp(s - m_blk)
    l_blk = jnp.sum(p, axis=1, keepdims=True)
    p = p.astype(BF16)
    part_ref[n] = (_dot_nt(p[:, 0:PAGE_SIZE], v0_ref[...].astype(BF16))
                   + _dot_nt(p[:, PAGE_SIZE:], v1_ref[...].astype(BF16)))
    here = col == n
    g_ref[...] = jnp.where(here, gate, g_ref[...])
    m_ref[...] = jnp.where(here, m_blk, m_ref[...])
    l_ref[...] = jnp.where(here, l_blk, l_ref[...])

    @pl.when(n == n_blk - 1)
    def _():
        sel = _top_blocks(g_ref[...], col, n_blk)
        s_self = jnp.sum(qm.astype(F32) * kn_ref[0].astype(F32), axis=1, keepdims=True)
        m_all = m_ref[...]
        m_fin = jnp.maximum(jnp.max(jnp.where(sel > 0.0, m_all, NEG_INF), axis=1, keepdims=True), s_self)
        w = jnp.where(sel > 0.0, jnp.exp(m_all - m_fin), 0.0)
        w_self = jnp.exp(s_self - m_fin)
        l_fin = jnp.sum(w * l_ref[...], axis=1, keepdims=True) + w_self
        acc = w_self * jnp.broadcast_to(vn_ref[0].astype(F32), (MOBA_HEADS, D_MODEL))

        def add(j, a):
            wj = jnp.sum(jnp.where(col == j, w, 0.0), axis=1, keepdims=True)
            return a + wj * part_ref[j]

        acc = lax.fori_loop(0, n_blk, add, acc)
        accn = acc / l_fin
        row = lax.broadcasted_iota(jnp.int32, accn.shape, 0)
        lane = lax.broadcasted_iota(jnp.int32, accn.shape, 1)
        o = jnp.sum(jnp.where(lane // MOBA_DH == row, accn, 0.0), axis=0, keepdims=True)
        o_ref[0] = o.astype(BF16)


def _moba_decode_call(page_ids, q_s, k_new, v_new, cache_k, cache_v):
    db, n_pages = page_ids.shape
    n_blk = n_pages * PAGE_SIZE // MOBA_BLOCK
    seq3 = lambda b, n, pt: (b, 0, 0)

    def page(i):
        return pl.BlockSpec((None, D_MODEL, PAGE_SIZE), lambda b, n, pt: (pt[b, 2 * n + i], 0, 0))

    grid_spec = pltpu.PrefetchScalarGridSpec(
        num_scalar_prefetch=1,
        grid=(db, n_blk),
        in_specs=[pl.BlockSpec((1, 1, D_MODEL), seq3)] * 3 + [page(0), page(1), page(0), page(1)],
        out_specs=pl.BlockSpec((1, 1, D_MODEL), seq3),
        scratch_shapes=[pltpu.VMEM((MOBA_HEADS, LANES), F32), pltpu.VMEM((MOBA_HEADS, LANES), F32),
                        pltpu.VMEM((MOBA_HEADS, LANES), F32),
                        pltpu.VMEM((n_blk, MOBA_HEADS, D_MODEL), F32)],
    )
    return pl.pallas_call(
        _moba_decode_kernel,
        grid_spec=grid_spec,
        out_shape=jax.ShapeDtypeStruct((db, 1, D_MODEL), BF16),
        compiler_params=_cparams(("arbitrary", "arbitrary")),
        name="moba_attn_decode",
    )(page_ids, q_s, k_new, v_new, cache_k, cache_k, cache_v, cache_v)


def _layer_norm(z, g, b):
    mu = jnp.mean(z, axis=1, keepdims=True)
    zc = z - mu
    var = jnp.mean(zc * zc, axis=1, keepdims=True)
    return zc * lax.rsqrt(var + LN_EPS) * g + b


def _outproj_kernel(o_ref, h_ref, wo_ref, g_ref, b_ref, wr_ref, br_ref,
                    h1_ref, idx_ref, gate_ref, *, alpha):
    mix = jnp.dot(o_ref[...], wo_ref[...], preferred_element_type=F32)
    h1 = _layer_norm(alpha * h_ref[...] + mix, g_ref[...], b_ref[...])
    h1_ref[...] = h1
    logits = jnp.dot(h1, wr_ref[...], preferred_element_type=F32,
                     precision=lax.Precision.HIGHEST) + br_ref[...]
    lane = lax.broadcasted_iota(jnp.int32, logits.shape, 1)
    logits = jnp.where(lane < N_EXPERTS, logits, NEG_INF)
    vals, idxs = [], []
    for _ in range(TOP_K):
        mx = jnp.max(logits, axis=1, keepdims=True)
        ix = jnp.min(jnp.where(logits == mx, lane, LANES), axis=1, keepdims=True)
        vals.append(mx)
        idxs.append(ix)
        logits = jnp.where(lane == ix, NEG_INF, logits)
    es = [jnp.exp(v - vals[0]) for v in vals]
    tot = es[0] + es[1] + es[2] + es[3]
    for j in range(TOP_K):
        idx_ref[:, j:j + 1] = idxs[j]
        gate_ref[:, j:j + 1] = es[j] / tot


def _outproj_call(o_bf, h, wo_bf, ln_g, ln_b, wr_pad, br_pad, alpha):
    n = h.shape[0]
    row = lambda i: (i, 0)
    fix = lambda i: (0, 0)
    return pl.pallas_call(
        functools.partial(_outproj_kernel, alpha=alpha),
        grid=(n // TOK_TILE,),
        in_specs=[pl.BlockSpec((TOK_TILE, D_MODEL), row),
                  pl.BlockSpec((TOK_TILE, D_MODEL), row),
                  pl.BlockSpec((D_MODEL, D_MODEL), fix),
                  pl.BlockSpec((1, D_MODEL), fix),
                  pl.BlockSpec((1, D_MODEL), fix),
                  pl.BlockSpec((D_MODEL, LANES), fix),
                  pl.BlockSpec((1, LANES), fix)],
        out_specs=[pl.BlockSpec((TOK_TILE, D_MODEL), row),
                   pl.BlockSpec((TOK_TILE, TOP_K), row),
                   pl.BlockSpec((TOK_TILE, TOP_K), row)],
        out_shape=[jax.ShapeDtypeStruct((n, D_MODEL), F32),
                   jax.ShapeDtypeStruct((n, TOP_K), jnp.int32),
                   jax.ShapeDtypeStruct((n, TOP_K), F32)],
        compiler_params=_cparams(("arbitrary",)),
        name="outproj_ln_router",
    )(o_bf, h, wo_bf, ln_g, ln_b, wr_pad, br_pad)


def _route_kernel(idx_ref, rank_ref, cnt_ref, run_ref, *, n_tok):
    i = pl.program_id(0)
    t = TOK_TILE

    @pl.when(i == 0)
    def _():
        run_ref[...] = jnp.zeros(run_ref.shape, F32)

    idx = idx_ref[...]
    lane = lax.broadcasted_iota(jnp.int32, (t, LANES), 1)
    live = i * t + lax.broadcasted_iota(jnp.int32, (t, LANES), 0) < n_tok
    hits = [(lane == idx[:, j:j + 1]) & live for j in range(TOP_K)]
    chose = jnp.where(hits[0], 1.0, 0.0)
    for j in range(1, TOP_K):
        chose = chose + jnp.where(hits[j], 1.0, 0.0)
    r = lax.broadcasted_iota(jnp.int32, (t, t), 0)
    c = lax.broadcasted_iota(jnp.int32, (t, t), 1)
    tri = jnp.where(c <= r, 1.0, 0.0).astype(BF16)
    incl = jnp.dot(tri, chose.astype(BF16), preferred_element_type=F32)
    before = run_ref[...] + incl - chose
    for j in range(TOP_K):
        rank_ref[:, j:j + 1] = jnp.sum(jnp.where(hits[j], before, 0.0), axis=1,
                                       keepdims=True).astype(jnp.int32)
    run_ref[...] = run_ref[...] + incl[t - 1:t, :]
    cnt_ref[...] = run_ref[...].astype(jnp.int32)


def _route_call(top_idx, n_tok):
    n = top_idx.shape[0]
    return pl.pallas_call(
        functools.partial(_route_kernel, n_tok=n_tok),
        grid=(n // TOK_TILE,),
        in_specs=[pl.BlockSpec((TOK_TILE, TOP_K), lambda i: (i, 0))],
        out_specs=[pl.BlockSpec((TOK_TILE, TOP_K), lambda i: (i, 0)),
                   pl.BlockSpec((1, LANES), lambda i: (0, 0))],
        out_shape=[jax.ShapeDtypeStruct((n, TOP_K), jnp.int32),
                   jax.ShapeDtypeStruct((1, LANES), jnp.int32)],
        scratch_shapes=[pltpu.VMEM((1, LANES), F32)],
        compiler_params=_cparams(("arbitrary",)),
        name="route_rank",
    )(top_idx)


def _route_tables(top_idx, rank, counts, n_tok, n_pad):
    counts = counts[0, :N_EXPERTS]
    padded = (counts + MOE_TILE - 1) // MOE_TILE * MOE_TILE
    ends = jnp.cumsum(padded)
    start = ends - padded
    n_asg = n_tok * TOP_K
    n_spare = (n_pad - n_tok) * TOP_K
    n_tiles = -(-n_asg // MOE_TILE) + N_EXPERTS + -(-n_spare // MOE_TILE)
    experts = jnp.arange(N_EXPERTS, dtype=jnp.int32)
    dest = jnp.sum(jnp.where(top_idx[..., None] == experts, start, 0), axis=-1) + rank
    flat = jnp.arange(n_pad * TOP_K, dtype=jnp.int32).reshape(n_pad, TOP_K)
    spare = n_tiles * MOE_TILE - n_spare + (flat - n_asg)
    dest = jnp.where(flat < n_asg, dest, spare).astype(jnp.int32)
    tile_start = jnp.arange(n_tiles, dtype=jnp.int32) * MOE_TILE
    tile_e = jnp.minimum(jnp.sum(tile_start[:, None] >= ends[None, :], axis=1), N_EXPERTS - 1).astype(jnp.int32)
    n_used = (ends[-1:] // MOE_TILE).astype(jnp.int32)
    nt = n_pad // TOK_TILE
    by_token = dest.reshape(nt, 1, TOK_TILE * TOP_K)
    by_slot = dest.reshape(nt, TOK_TILE, TOP_K).transpose(0, 2, 1).reshape(nt, 1, TOK_TILE * TOP_K)
    return tile_e, n_used, n_tiles, by_token, by_slot


def _scatter_kernel(dest_ref, h1_ref, xs_in_hbm, xs_hbm, sem):
    del xs_in_hbm

    def row_copy(t, dst_row):
        return pltpu.make_async_copy(h1_ref.at[pl.ds(t, 1)], xs_hbm.at[pl.ds(dst_row, 1)], sem)

    def start(t, c):
        for j in range(TOP_K):
            row_copy(t, dest_ref[0, 0, t * TOP_K + j]).start()
        return c

    lax.fori_loop(0, TOK_TILE, start, 0, unroll=2)

    def wait(t, c):
        for j in range(TOP_K):
            row_copy(t, 0).wait()
        return c

    lax.fori_loop(0, TOK_TILE, wait, 0, unroll=2)


def _scatter_call(by_token, h1, n_rows):
    nt = h1.shape[0] // TOK_TILE
    return pl.pallas_call(
        _scatter_kernel,
        grid=(nt,),
        in_specs=[pl.BlockSpec((1, 1, TOK_TILE * TOP_K), lambda i: (i, 0, 0), memory_space=pltpu.SMEM),
                  pl.BlockSpec((TOK_TILE, D_MODEL), lambda i: (i, 0)),
                  pl.BlockSpec(memory_space=pl.ANY)],
        out_specs=pl.BlockSpec(memory_space=pl.ANY),
        out_shape=jax.ShapeDtypeStruct((n_rows, D_MODEL), F32),
        scratch_shapes=[pltpu.SemaphoreType.DMA(())],
        input_output_aliases={2: 0},
        compiler_params=_cparams(("arbitrary",)),
        name="moe_scatter_rows",
    )(by_token, h1, jnp.zeros((n_rows, D_MODEL), F32))


def _moe_kernel(te_ref, nu_ref, x_ref, wgu_ref, bgu_ref, wdn_ref, bdn_ref, y_ref, wgu_bf, wdn_bf):
    i = pl.program_id(0)
    live = i < nu_ref[0]
    new_expert = (i == 0) | (te_ref[i] != te_ref[jnp.maximum(i - 1, 0)])

    @pl.when(live & new_expert)
    def _():
        wgu_bf[...] = wgu_ref[0].astype(BF16)
        wdn_bf[...] = wdn_ref[0].astype(BF16)

    @pl.when(live)
    def _():
        x = x_ref[...].astype(BF16)
        h = jnp.dot(x, wgu_bf[...], preferred_element_type=F32) + bgu_ref[0]
        gate = jnp.minimum(h[:, :D_FF], SWIGLU_LIMIT)
        up = jnp.clip(h[:, D_FF:], -SWIGLU_LIMIT, SWIGLU_LIMIT)
        glu = gate * (1.0 / (1.0 + jnp.exp(-SWIGLU_ALPHA * gate)))
        act = ((up + 1.0) * glu).astype(BF16)
        y_ref[...] = jnp.dot(act, wdn_bf[...], preferred_element_type=F32) + bdn_ref[0]

    @pl.when(jnp.logical_not(live))
    def _():
        y_ref[...] = jnp.zeros(y_ref.shape, F32)


def _moe_call(tile_e, n_used, xs, layer, w_gu, b_gu, w_dn, b_dn):
    n_tiles = tile_e.shape[0]

    def tile(i, te, nu):
        return jnp.minimum(i, nu[0] - 1)

    def wmap(i, te, nu):
        return (layer, te[tile(i, te, nu)], 0, 0)

    grid_spec = pltpu.PrefetchScalarGridSpec(
        num_scalar_prefetch=2,
        grid=(n_tiles,),
        in_specs=[pl.BlockSpec((MOE_TILE, D_MODEL), lambda i, te, nu: (tile(i, te, nu), 0)),
                  pl.BlockSpec((None, 1, D_MODEL, 2 * D_FF), wmap),
                  pl.BlockSpec((None, 1, 1, 2 * D_FF), wmap),
                  pl.BlockSpec((None, 1, D_FF, D_MODEL), wmap),
                  pl.BlockSpec((None, 1, 1, D_MODEL), wmap)],
        out_specs=pl.BlockSpec((MOE_TILE, D_MODEL), lambda i, te, nu: (i, 0)),
        scratch_shapes=[pltpu.VMEM((D_MODEL, 2 * D_FF), BF16), pltpu.VMEM((D_FF, D_MODEL), BF16)],
    )
    depth = w_gu.shape[0]
    return pl.pallas_call(
        _moe_kernel,
        grid_spec=grid_spec,
        out_shape=jax.ShapeDtypeStruct((n_tiles * MOE_TILE, D_MODEL), F32),
        compiler_params=_cparams(("arbitrary",), MOE_VMEM_LIMIT),
        name="moe_experts",
    )(tile_e, n_used, xs, w_gu, b_gu.reshape(depth, N_EXPERTS, 1, 2 * D_FF), w_dn,
      b_dn.reshape(depth, N_EXPERTS, 1, D_MODEL))


def _row_copy(src_hbm, dst_vmem, sem, src_row, dst_row):
    return pltpu.make_async_copy(src_hbm.at[pl.ds(src_row, 1)], dst_vmem.at[pl.ds(dst_row, 1)], sem)


def _combine_kernel(pos_ref, gates_ref, h1_ref, g_ref, b_ref, ys_hbm, h2_ref, buf, sem, *, alpha):
    n_rows = TOK_TILE * TOP_K

    def start(a, c):
        _row_copy(ys_hbm, buf, sem, pos_ref[0, 0, a], a).start()
        return c

    lax.fori_loop(0, n_rows, start, 0, unroll=8)

    def wait(a, c):
        _row_copy(ys_hbm, buf, sem, 0, a).wait()
        return c

    lax.fori_loop(0, n_rows, wait, 0, unroll=8)
    gates = gates_ref[...]
    y = gates[:, 0:1] * buf[0:TOK_TILE, :]
    for j in range(1, TOP_K):
        y += gates[:, j:j + 1] * buf[j * TOK_TILE:(j + 1) * TOK_TILE, :]
    h2_ref[...] = _layer_norm(alpha * h1_ref[...] + y, g_ref[...], b_ref[...])


def _combine_call(pos, gates, h1, ln_g, ln_b, ys, alpha):
    n = h1.shape[0]
    nt = n // TOK_TILE
    row = lambda i: (i, 0)
    fix = lambda i: (0, 0)
    return pl.pallas_call(
        functools.partial(_combine_kernel, alpha=alpha),
        grid=(nt,),
        in_specs=[pl.BlockSpec((1, 1, TOK_TILE * TOP_K), lambda i: (i, 0, 0), memory_space=pltpu.SMEM),
                  pl.BlockSpec((TOK_TILE, TOP_K), row),
                  pl.BlockSpec((TOK_TILE, D_MODEL), row),
                  pl.BlockSpec((1, D_MODEL), fix),
                  pl.BlockSpec((1, D_MODEL), fix),
                  pl.BlockSpec(memory_space=pl.ANY)],
        out_specs=pl.BlockSpec((TOK_TILE, D_MODEL), row),
        out_shape=jax.ShapeDtypeStruct((n, D_MODEL), F32),
        scratch_shapes=[pltpu.VMEM((TOK_TILE * TOP_K, D_MODEL), F32), pltpu.SemaphoreType.DMA(())],
        compiler_params=_cparams(("arbitrary",)),
        name="moe_combine_ln",
    )(pos, gates, h1, ln_g, ln_b, ys)


def kernel(x_prompt, x_sample, cache_k_diff, cache_v_diff, cache_k_moba, cache_v_moba, page_table,
           diff_w_qkv, diff_w_o, diff_lambda_q1, diff_lambda_k1, diff_lambda_q2, diff_lambda_k2,
           diff_subln_g, moba_w_qkv, moba_w_o, ln1_g, ln1_b, ln2_g, ln2_b, moe_w_router,
           moe_b_router, moe_w_gate_up, moe_b_gate_up, moe_w_down, moe_b_down):
    batch, seq, _ = x_prompt.shape
    db = x_sample.shape[0]
    depth = ln1_g.shape[0]
    n_prompt = batch * seq
    n_tok = n_prompt + db
    n_pad = -(-n_tok // TOK_TILE) * TOK_TILE
    past_len = page_table.shape[1] * PAGE_SIZE
    alpha = (2.0 * depth) ** 0.25

    h = jnp.concatenate([x_prompt.reshape(n_prompt, D_MODEL), x_sample.reshape(db, D_MODEL),
                         jnp.zeros((n_pad - n_tok, D_MODEL), F32)], axis=0)
    pos = jnp.concatenate([jnp.tile(jnp.arange(seq, dtype=jnp.int32), batch),
                           jnp.full((n_pad - n_prompt,), past_len, jnp.int32)])
    cos, sa, sb = _rope_tables(pos)
    pad_rows = jnp.zeros((n_pad - n_tok, D_MODEL), BF16)
    sample = slice(n_prompt, n_tok)

    def feature_major(c):
        perm = (0, 1) + tuple(range(3, c.ndim)) + (2,)
        return jnp.transpose(c, perm).reshape(c.shape[0] * c.shape[1], D_MODEL, PAGE_SIZE)

    outs = {name: [] for name in ("kd_p", "vd_p", "km_p", "vm_p", "kd_s", "vd_s", "km_s", "vm_s")}
    for i in range(depth):
        j = i // 2
        is_diff = i % 2 == 0
        w_qkv = (diff_w_qkv if is_diff else moba_w_qkv)[j].astype(BF16)
        w_o = (diff_w_o if is_diff else moba_w_o)[j].astype(BF16)
        q_bf, k, v, k_bf, v_bf, kmean = _proj_call(h, w_qkv, cos, sa, sb)
        q_s = q_bf[sample].reshape(db, 1, D_MODEL)
        k_s = k_bf[sample].reshape(db, 1, D_MODEL)
        v_s = v_bf[sample].reshape(db, 1, D_MODEL)
        if is_diff:
            lam_init = 0.8 - 0.6 * math.exp(-0.3 * i)
            lams = [p[j].astype(F32).reshape(1, DIFF_DK)
                    for p in (diff_lambda_q1, diff_lambda_k1, diff_lambda_q2, diff_lambda_k2)]
            g = diff_subln_g[j].astype(F32).reshape(1, LANES)
            pages = page_table + j * cache_k_diff.shape[1]
            o_p = _diff_attn_call(q_bf, k_bf, v_bf, lams, g, batch, seq, lam_init)
            v_pages = cache_v_diff.reshape(-1, PAGE_SIZE * DIFF_HEADS, LANES)
            o_s = _diff_decode_call(pages, q_s, k_s, v_s, feature_major(cache_k_diff), v_pages,
                                    lams, g, lam_init)
            pre = "d"
            kshape, vshape = (DIFF_HEADS, 2, DIFF_DK), (DIFF_HEADS, 2 * DIFF_DK)
        else:
            pages = page_table + j * cache_k_moba.shape[1]
            o_p = _moba_attn_call(q_bf, k_bf, v_bf, kmean.reshape(-1, D_MODEL), batch, seq)
            o_s = _moba_decode_call(pages, q_s, k_s, v_s, feature_major(cache_k_moba),
                                    feature_major(cache_v_moba))
            pre = "m"
            kshape, vshape = (MOBA_HEADS, MOBA_DH), (MOBA_HEADS, MOBA_DH)
        outs[f"k{pre}_p"].append(k[:n_prompt].reshape((batch, seq) + kshape))
        outs[f"v{pre}_p"].append(v[:n_prompt].reshape((batch, seq) + vshape))
        outs[f"k{pre}_s"].append(k[sample].reshape((db, 1) + kshape))
        outs[f"v{pre}_s"].append(v[sample].reshape((db, 1) + vshape))

        o_all = jnp.concatenate([o_p, o_s.reshape(db, D_MODEL), pad_rows], axis=0)
        wr_pad = jnp.zeros((D_MODEL, LANES), F32).at[:, :N_EXPERTS].set(moe_w_router[i].astype(F32))
        br_pad = jnp.zeros((1, LANES), F32).at[0, :N_EXPERTS].set(moe_b_router[i].astype(F32))
        h1, top_idx, gates = _outproj_call(o_all, h, w_o, ln1_g[i].reshape(1, D_MODEL),
                                           ln1_b[i].reshape(1, D_MODEL), wr_pad, br_pad, alpha)
        rank, counts = _route_call(top_idx, n_tok)
        tile_e, n_used, n_tiles, by_token, by_slot = _route_tables(top_idx, rank, counts, n_tok, n_pad)
        xs = _scatter_call(by_token, h1, n_tiles * MOE_TILE)
        ys = _moe_call(tile_e, n_used, xs, i, moe_w_gate_up, moe_b_gate_up, moe_w_down, moe_b_down)
        h = _combine_call(by_slot, gates, h1, ln2_g[i].reshape(1, D_MODEL), ln2_b[i].reshape(1, D_MODEL),
                          ys, alpha)

    y_prompt = h[:n_prompt].reshape(batch, seq, D_MODEL)
    y_sample = h[sample].reshape(db, 1, D_MODEL)
    return (y_prompt, y_sample, jnp.stack(outs["kd_p"]), jnp.stack(outs["vd_p"]),
            jnp.stack(outs["km_p"]), jnp.stack(outs["vm_p"]), jnp.stack(outs["kd_s"]),
            jnp.stack(outs["vd_s"]), jnp.stack(outs["km_s"]), jnp.stack(outs["vm_s"]))
```

```python
import functools
import math

import jax
import jax.numpy as jnp
from jax import lax
from jax.experimental import pallas as pl
from jax.experimental.pallas import tpu as pltpu

F32 = jnp.float32
BF16 = jnp.bfloat16
NEG_INF = float("-inf")

D_MODEL = 1024
PAGE_SIZE = 128
DIFF_HEADS = 8
DIFF_DK = 64
MOBA_HEADS = 16
MOBA_DH = 64
MOBA_BLOCK = 256
MOBA_TOPK = 3
N_EXPERTS = 32
TOP_K = 4
D_FF = D_MODEL
SWIGLU_LIMIT = 7.0
SWIGLU_ALPHA = 1.702
ROPE_THETA = 10000.0
LN_EPS = 1e-5
HEAD_DIM = 64
Q_SCALE = 1.0 / math.sqrt(HEAD_DIM)

LANES = 128
N_COLBLK = D_MODEL // LANES
TOK_TILE = 256
DIFF_TQ = 512
MOE_TILE = 256
DECODE_PAGES = 8
MOBA_DECODE_PAGES = 8
MOBA_KV_BLOCKS = 4
VMEM_LIMIT = 48 * 1024 * 1024
MOE_VMEM_LIMIT = 56 * 1024 * 1024


def _cparams(sem, vmem=VMEM_LIMIT):
    return pltpu.CompilerParams(dimension_semantics=sem, vmem_limit_bytes=vmem)


def _rope_tables(pos):
    inv = ROPE_THETA ** (-jnp.arange(0, HEAD_DIM, 2, dtype=F32) / HEAD_DIM)
    ang = pos.astype(F32)[:, None] * inv[None, :]
    ang = jnp.tile(ang, (1, LANES // (HEAD_DIM // 2)))
    cos, sin = jnp.cos(ang), jnp.sin(ang)
    first = (jnp.arange(LANES) % HEAD_DIM) < HEAD_DIM // 2
    return cos, jnp.where(first, -sin, 0.0), jnp.where(first, 0.0, sin)


def _proj_kernel(x_ref, w_ref, cos_ref, sa_ref, sb_ref,
                 q_ref, kb_ref, vb_ref, kt_ref, vp_ref, ks_ref, vs_ref, km_ref, *, n_prompt_tiles, v_feature_major):
    i = pl.program_id(0)
    xb = x_ref[...].astype(BF16)
    cos, sa, sb = cos_ref[...], sa_ref[...], sb_ref[...]

    def rope(a):
        half = HEAD_DIM // 2
        return a * cos + pltpu.roll(a, LANES - half, 1) * sa + pltpu.roll(a, half, 1) * sb

    aq = jnp.dot(xb, w_ref[:, 0:D_MODEL], preferred_element_type=F32)
    for c in range(N_COLBLK):
        sl = slice(c * LANES, (c + 1) * LANES)
        q_ref[:, sl] = (rope(aq[:, sl]) * Q_SCALE).astype(BF16)
    ak = jnp.dot(xb, w_ref[:, D_MODEL:2 * D_MODEL], preferred_element_type=F32)
    kr = []
    for c in range(N_COLBLK):
        sl = slice(c * LANES, (c + 1) * LANES)
        kr.append(rope(ak[:, sl]))
        kb_ref[:, sl] = kr[c].astype(BF16)
        km_ref[0, :, sl] = jnp.mean(kr[c], axis=0, keepdims=True)
    av = jnp.dot(xb, w_ref[:, 2 * D_MODEL:3 * D_MODEL], preferred_element_type=F32)
    vb_ref[...] = av.astype(BF16)

    @pl.when(i < n_prompt_tiles)
    def _():
        for c in range(N_COLBLK):
            sl = slice(c * LANES, (c + 1) * LANES)
            kt_ref[sl, :] = kr[c].T
            if v_feature_major:
                vp_ref[sl, :] = av[:, sl].T
        if not v_feature_major:
            vp_ref[...] = av

    @pl.when(i == n_prompt_tiles)
    def _():
        ks_ref[...] = jnp.concatenate(kr, axis=1)
        vs_ref[...] = av


def _proj_call(x, w_bf, cos, sa, sb, batch, seq, v_feature_major):
    n = x.shape[0]
    nt = n // TOK_TILE
    tiles_per_seq = seq // TOK_TILE
    npt = batch * tiles_per_seq
    assert nt == npt + 1, "sample tokens and padding must fit one token tile"
    row = lambda i: (i, 0)
    fix = lambda i: (0, 0)

    def feat(i):
        j = jnp.minimum(i, npt - 1)
        return (j // tiles_per_seq, 0, j % tiles_per_seq)

    feat_spec = pl.BlockSpec((None, D_MODEL, TOK_TILE), feat)
    feat_shape = jax.ShapeDtypeStruct((batch, D_MODEL, seq), F32)
    if v_feature_major:
        v_spec, v_shape = feat_spec, feat_shape
    else:
        v_spec = pl.BlockSpec((TOK_TILE, D_MODEL), lambda i: (jnp.minimum(i, npt - 1), 0))
        v_shape = jax.ShapeDtypeStruct((batch * seq, D_MODEL), F32)
    return pl.pallas_call(
        functools.partial(_proj_kernel, n_prompt_tiles=npt, v_feature_major=v_feature_major),
        grid=(nt,),
        in_specs=[pl.BlockSpec((TOK_TILE, D_MODEL), row),
                  pl.BlockSpec((D_MODEL, 3 * D_MODEL), fix),
                  pl.BlockSpec((TOK_TILE, LANES), row),
                  pl.BlockSpec((TOK_TILE, LANES), row),
                  pl.BlockSpec((TOK_TILE, LANES), row)],
        out_specs=[pl.BlockSpec((TOK_TILE, D_MODEL), row),
                   pl.BlockSpec((TOK_TILE, D_MODEL), row),
                   pl.BlockSpec((TOK_TILE, D_MODEL), row),
                   feat_spec, v_spec,
                   pl.BlockSpec((TOK_TILE, D_MODEL), fix),
                   pl.BlockSpec((TOK_TILE, D_MODEL), fix),
                   pl.BlockSpec((1, 1, D_MODEL), lambda i: (i, 0, 0))],
        out_shape=[jax.ShapeDtypeStruct((n, D_MODEL), BF16),
                   jax.ShapeDtypeStruct((n, D_MODEL), BF16),
                   jax.ShapeDtypeStruct((n, D_MODEL), BF16),
                   feat_shape, v_shape,
                   jax.ShapeDtypeStruct((TOK_TILE, D_MODEL), F32),
                   jax.ShapeDtypeStruct((TOK_TILE, D_MODEL), F32),
                   jax.ShapeDtypeStruct((nt, 1, D_MODEL), F32)],
        compiler_params=_cparams(("arbitrary",)),
        name="qkv_rope",
    )(x, w_bf, cos, sa, sb)


def _half_masks(q):
    lane = lax.broadcasted_iota(jnp.int32, q.shape, 1)
    qf = q.astype(F32)
    return (jnp.where(lane < HEAD_DIM, qf, 0.0).astype(q.dtype),
            jnp.where(lane >= HEAD_DIM, qf, 0.0).astype(q.dtype))


def _dot_nt(a, b):
    return lax.dot_general(a, b, (((1,), (1,)), ((), ())), preferred_element_type=F32)


def _with_ones(vt):
    return jnp.concatenate([vt, jnp.ones(vt.shape, vt.dtype)], axis=1)


def _flash_update(c, s, v_aug, m_ref, acc_ref):
    m_prev = m_ref[c]
    m_new = jnp.maximum(m_prev, jnp.max(s, axis=1, keepdims=True))
    alpha = jnp.exp(m_prev - m_new)
    p = jnp.concatenate([jnp.exp(s[:, j * LANES:(j + 1) * LANES] - m_new)
                         for j in range(s.shape[1] // LANES)], axis=1)
    acc_ref[c] = (jnp.concatenate([alpha, alpha], axis=1) * acc_ref[c]
                  + jnp.dot(p.astype(BF16), v_aug, preferred_element_type=F32))
    m_ref[c] = m_new


def _flash_result(c, acc_ref):
    acc = acc_ref[c]
    return acc[:, :LANES] / acc[:, LANES:]


def _diff_lambda(lq1, lk1, lq2, lk2, lam_init):
    return (jnp.exp(jnp.sum(lq1 * lk1, axis=1, keepdims=True))
            - jnp.exp(jnp.sum(lq2 * lk2, axis=1, keepdims=True)) + lam_init)


def _diff_attn_kernel(q_ref, k_ref, v_ref, lq1_ref, lk1_ref, lq2_ref, lk2_ref, g_ref,
                      o_ref, m_ref, acc_ref, sa_ref, sb_ref, *, lam_init):
    t = DIFF_TQ
    qi = pl.program_id(2)
    q_maps = _half_masks(q_ref[...])
    m_ref[...] = jnp.full(m_ref.shape, NEG_INF, F32)
    acc_ref[...] = jnp.zeros(acc_ref.shape, F32)

    def scores(kj, s_ref):
        kt = k_ref[pl.ds(pl.multiple_of(kj * t, t), t), :]
        for c in range(2):
            s_ref[c] = _dot_nt(q_maps[c], kt)

    def consume(kj, s_ref, causal):
        v_aug = _with_ones(v_ref[pl.ds(pl.multiple_of(kj * t, t), t), :])
        for c in range(2):
            s = s_ref[c]
            if causal:
                row = lax.broadcasted_iota(jnp.int32, s.shape, 0)
                col = lax.broadcasted_iota(jnp.int32, s.shape, 1)
                s = jnp.where(col <= row, s, NEG_INF)
            _flash_update(c, s, v_aug, m_ref, acc_ref)

    scores(0, sa_ref)

    def pair(mi, carry):
        kj = 2 * mi
        scores(kj + 1, sb_ref)
        consume(kj, sa_ref, False)
        scores(kj + 2, sa_ref)
        consume(kj + 1, sb_ref, False)
        return carry

    lax.fori_loop(0, qi // 2, pair, 0)

    @pl.when(qi % 2 == 0)
    def _():
        consume(qi, sa_ref, True)

    @pl.when(qi % 2 == 1)
    def _():
        scores(qi, sb_ref)
        consume(qi - 1, sa_ref, False)
        consume(qi, sb_ref, True)

    lam = _diff_lambda(lq1_ref[...], lk1_ref[...], lq2_ref[...], lk2_ref[...], lam_init)
    o = _flash_result(0, acc_ref) - lam * _flash_result(1, acc_ref)
    ms = jnp.mean(o * o, axis=1, keepdims=True)
    o = o * lax.rsqrt(ms + LN_EPS) * g_ref[...] * (1.0 - lam_init)
    o_ref[...] = o.astype(BF16)


def _diff_attn_call(q_bf, k_bf, v_bf, lams, g, batch, seq, lam_init):
    t = DIFF_TQ
    nq = seq // t
    qmap = lambda b, h, i: (b * nq + i, h)
    kvmap = lambda b, h, i: (b, h)
    fix = lambda b, h, i: (0, 0)
    return pl.pallas_call(
        functools.partial(_diff_attn_kernel, lam_init=lam_init),
        grid=(batch, DIFF_HEADS, nq),
        in_specs=[pl.BlockSpec((t, LANES), qmap),
                  pl.BlockSpec((seq, LANES), kvmap),
                  pl.BlockSpec((seq, LANES), kvmap)]
                 + [pl.BlockSpec((1, DIFF_DK), fix)] * 4
                 + [pl.BlockSpec((1, LANES), fix)],
        out_specs=pl.BlockSpec((t, LANES), qmap),
        out_shape=jax.ShapeDtypeStruct((batch * seq, D_MODEL), BF16),
        scratch_shapes=[pltpu.VMEM((2, t, LANES), F32), pltpu.VMEM((2, t, 2 * LANES), F32),
                        pltpu.VMEM((2, t, t), F32), pltpu.VMEM((2, t, t), F32)],
        compiler_params=_cparams(("arbitrary", "arbitrary", "arbitrary")),
        name="diff_attn_prompt",
    )(q_bf, k_bf, v_bf, *lams, g)


def _row_query(q, width):
    rows = D_MODEL // width
    lane = lax.broadcasted_iota(jnp.int32, (rows, D_MODEL), 1)
    row = lax.broadcasted_iota(jnp.int32, (rows, D_MODEL), 0)
    qb = jnp.broadcast_to(q.astype(F32), (rows, D_MODEL))
    return jnp.where(lane // width == row, qb, 0.0).astype(q.dtype)


def _map_major_query(q):
    rows = 2 * DIFF_HEADS
    lane = lax.broadcasted_iota(jnp.int32, (rows, D_MODEL), 1)
    row = lax.broadcasted_iota(jnp.int32, (rows, D_MODEL), 0)
    qb = jnp.broadcast_to(q.astype(F32), (rows, D_MODEL))
    chunk = 2 * (row % DIFF_HEADS) + row // DIFF_HEADS
    return jnp.where(lane // DIFF_DK == chunk, qb, 0.0).astype(q.dtype)


def _diff_decode_kernel(pt_ref, q_ref, kn_ref, vn_ref, e_ref, *rest, lam_init):
    np_ = DECODE_PAGES
    k_refs, v_refs = rest[:np_], rest[np_:2 * np_]
    lq1_ref, lk1_ref, lq2_ref, lk2_ref, g_ref, o_ref, m_ref, l_ref, acc_ref = rest[2 * np_:]
    g = pl.program_id(1)
    rows = 2 * DIFF_HEADS
    qm = _map_major_query(q_ref[0])

    @pl.when(g == 0)
    def _():
        s_self = jnp.sum(qm.astype(F32) * kn_ref[0].astype(F32), axis=1, keepdims=True)
        m_ref[...] = s_self
        l_ref[...] = jnp.ones(l_ref.shape, F32)
        vn = vn_ref[0].astype(F32)
        acc_ref[...] = jnp.concatenate([vn, vn], axis=0)

    s = jnp.concatenate([jnp.dot(qm, k_refs[i][...].astype(BF16), preferred_element_type=F32)
                         for i in range(np_)], axis=1)
    m_prev = m_ref[...]
    m_new = jnp.maximum(m_prev, jnp.max(s, axis=1, keepdims=True))
    alpha = jnp.exp(m_prev - m_new)
    p = jnp.exp(s - m_new)
    l_ref[...] = alpha * l_ref[...] + jnp.sum(p, axis=1, keepdims=True)
    p = p.astype(BF16)
    p_rows = jnp.concatenate([p[:, i * PAGE_SIZE:(i + 1) * PAGE_SIZE] for i in range(np_)], axis=0)
    spread = jnp.dot(p_rows, e_ref[...], preferred_element_type=F32)
    srow = lax.broadcasted_iota(jnp.int32, spread.shape, 0)
    slane = lax.broadcasted_iota(jnp.int32, spread.shape, 1)
    spread = jnp.where(slane % DIFF_HEADS == srow % DIFF_HEADS, spread, 0.0).astype(BF16)
    pv = None
    for i in range(np_):
        part = jnp.dot(spread[i * rows:(i + 1) * rows, :], v_refs[i][...].astype(BF16),
                       preferred_element_type=F32)
        pv = part if pv is None else pv + part
    acc_ref[...] = alpha * acc_ref[...] + pv
    m_ref[...] = m_new

    @pl.when(g == pl.num_programs(1) - 1)
    def _():
        accn = acc_ref[...] / l_ref[...]
        lam = _diff_lambda(lq1_ref[...], lk1_ref[...], lq2_ref[...], lk2_ref[...], lam_init)
        o = accn[0:DIFF_HEADS] - lam * accn[DIFF_HEADS:rows]
        ms = jnp.mean(o * o, axis=1, keepdims=True)
        o_ref[0] = (o * lax.rsqrt(ms + LN_EPS) * g_ref[...] * (1.0 - lam_init)).astype(BF16)


def _diff_decode_call(page_ids, q_s, k_new, v_new, cache_k, cache_v, lams, g, lam_init):
    db, n_pages = page_ids.shape
    np_ = DECODE_PAGES
    seq3 = lambda b, g_, pt: (b, 0, 0)
    fix = lambda b, g_, pt: (0, 0)
    tok = jnp.arange(PAGE_SIZE, dtype=jnp.int32)[:, None]
    rowid = jnp.arange(PAGE_SIZE * DIFF_HEADS, dtype=jnp.int32)[None, :]
    spread_mat = (rowid // DIFF_HEADS == tok).astype(BF16)

    def page(i):
        return pl.BlockSpec((None, D_MODEL, PAGE_SIZE), lambda b, g_, pt: (pt[b, g_ * np_ + i], 0, 0))

    grid_spec = pltpu.PrefetchScalarGridSpec(
        num_scalar_prefetch=1,
        grid=(db, n_pages // np_),
        in_specs=[pl.BlockSpec((1, 1, D_MODEL), seq3)] * 2
                 + [pl.BlockSpec((1, DIFF_HEADS, LANES), seq3)]
                 + [pl.BlockSpec((PAGE_SIZE, D_MODEL), fix)]
                 + [page(i) for i in range(np_)] + [page(i) for i in range(np_)]
                 + [pl.BlockSpec((1, DIFF_DK), fix)] * 4
                 + [pl.BlockSpec((1, LANES), fix)],
        out_specs=pl.BlockSpec((1, DIFF_HEADS, LANES), seq3),
        scratch_shapes=[pltpu.VMEM((2 * DIFF_HEADS, 1), F32), pltpu.VMEM((2 * DIFF_HEADS, 1), F32),
                        pltpu.VMEM((2 * DIFF_HEADS, LANES), F32)],
    )
    return pl.pallas_call(
        functools.partial(_diff_decode_kernel, lam_init=lam_init),
        grid_spec=grid_spec,
        out_shape=jax.ShapeDtypeStruct((db, DIFF_HEADS, LANES), BF16),
        compiler_params=_cparams(("arbitrary", "arbitrary")),
        name="diff_attn_decode",
    )(page_ids, q_s, k_new, v_new.reshape(db, DIFF_HEADS, LANES), spread_mat,
      *([cache_k] * np_), *([cache_v] * np_), *lams, g)


def _top_blocks(g, blk, n_valid):
    g = jnp.where(blk < n_valid, g, NEG_INF)
    n_blk = g.shape[1]
    sel = jnp.zeros(g.shape, F32)
    for _ in range(MOBA_TOPK):
        mx = jnp.max(g, axis=1, keepdims=True)
        ix = jnp.min(jnp.where(g == mx, blk, n_blk), axis=1, keepdims=True)
        pick = blk == ix
        sel = jnp.where(pick & (ix < n_valid), 1.0, sel)
        g = jnp.where(pick, NEG_INF, g)
    return sel


def _moba_attn_kernel(q_ref, k_ref, v_ref, km_ref, o_ref, m_ref, acc_ref, sel_ref, sa_ref, sb_ref):
    t = MOBA_BLOCK
    nb = MOBA_KV_BLOCKS
    w = nb * t
    qi = pl.program_id(2)
    n_chunks = (qi + nb - 1) // nb
    last_chunk = k_ref.shape[0] // w - 1
    q_heads = _half_masks(q_ref[...])
    km = km_ref[...].astype(BF16)
    blk = lax.broadcasted_iota(jnp.int32, (t, km.shape[0]), 1)

    def scores(kj, s_ref):
        kj = jnp.minimum(kj, last_chunk)
        kt = k_ref[pl.ds(pl.multiple_of(kj * w, w), w), :]
        for c in range(2):
            s_ref[c] = _dot_nt(q_heads[c], kt)

    def consume(kj, s_ref):
        v_aug = _with_ones(v_ref[pl.ds(pl.multiple_of(kj * w, w), w), :])
        for c in range(2):
            s = s_ref[c]
            parts = []
            for i in range(nb):
                chosen = jnp.max(jnp.where(blk == kj * nb + i, sel_ref[c], 0.0), axis=1, keepdims=True)
                parts.append(jnp.where(chosen > 0.0, s[:, i * t:(i + 1) * t], NEG_INF))
            _flash_update(c, jnp.concatenate(parts, axis=1), v_aug, m_ref, acc_ref)

    scores(0, sa_ref)
    for c in range(2):
        sel_ref[c] = _top_blocks(_dot_nt(q_heads[c], km), blk, qi)
    m_ref[...] = jnp.full(m_ref.shape, NEG_INF, F32)
    acc_ref[...] = jnp.zeros(acc_ref.shape, F32)

    start = pl.multiple_of(qi * t, t)
    kt = k_ref[pl.ds(start, t), :]
    v_aug = _with_ones(v_ref[pl.ds(start, t), :])
    row = lax.broadcasted_iota(jnp.int32, (t, t), 0)
    col = lax.broadcasted_iota(jnp.int32, (t, t), 1)
    for c in range(2):
        s = jnp.where(col <= row, _dot_nt(q_heads[c], kt), NEG_INF)
        _flash_update(c, s, v_aug, m_ref, acc_ref)

    def pair(mi, carry):
        kj = 2 * mi
        scores(kj + 1, sb_ref)
        consume(kj, sa_ref)
        scores(kj + 2, sa_ref)
        consume(kj + 1, sb_ref)
        return carry

    lax.fori_loop(0, n_chunks // 2, pair, 0)

    @pl.when(n_chunks % 2 == 1)
    def _():
        consume(n_chunks - 1, sa_ref)

    lane = lax.broadcasted_iota(jnp.int32, (t, LANES), 1)
    o = jnp.where(lane < HEAD_DIM, _flash_result(0, acc_ref), _flash_result(1, acc_ref))
    o_ref[...] = o.astype(BF16)


def _moba_attn_call(q_bf, k_bf, v_bf, kmean, batch, seq):
    t = MOBA_BLOCK
    nq = seq // t
    qmap = lambda b, h, i: (b * nq + i, h)
    kvmap = lambda b, h, i: (b, h)
    return pl.pallas_call(
        _moba_attn_kernel,
        grid=(batch, MOBA_HEADS // 2, nq),
        in_specs=[pl.BlockSpec((t, LANES), qmap),
                  pl.BlockSpec((seq, LANES), kvmap),
                  pl.BlockSpec((seq, LANES), kvmap),
                  pl.BlockSpec((nq, LANES), kvmap)],
        out_specs=pl.BlockSpec((t, LANES), qmap),
        out_shape=jax.ShapeDtypeStruct((batch * seq, D_MODEL), BF16),
        scratch_shapes=[pltpu.VMEM((2, t, LANES), F32), pltpu.VMEM((2, t, 2 * LANES), F32),
                        pltpu.VMEM((2, t, nq), F32),
                        pltpu.VMEM((2, t, MOBA_KV_BLOCKS * t), F32),
                        pltpu.VMEM((2, t, MOBA_KV_BLOCKS * t), F32)],
        compiler_params=_cparams(("arbitrary", "arbitrary", "arbitrary")),
        name="moba_attn_prompt",
    )(q_bf, k_bf, v_bf, kmean)


def _moba_decode_kernel(pt_ref, q_ref, kn_ref, vn_ref, *rest):
    npg = MOBA_DECODE_PAGES
    pages_per_blk = MOBA_BLOCK // PAGE_SIZE
    k_refs, v_refs = rest[:npg], rest[npg:2 * npg]
    o_ref, g_ref, m_ref, l_ref, part_ref = rest[2 * npg:]
    step = pl.program_id(1)
    blks = npg // pages_per_blk
    n_blk = pl.num_programs(1) * blks
    qm = _row_query(q_ref[0], MOBA_DH)
    col = lax.broadcasted_iota(jnp.int32, g_ref.shape, 1)

    @pl.when(step == 0)
    def _():
        g_ref[...] = jnp.full(g_ref.shape, NEG_INF, F32)
        m_ref[...] = jnp.zeros(m_ref.shape, F32)
        l_ref[...] = jnp.zeros(l_ref.shape, F32)

    for b in range(blks):
        n = step * blks + b
        pages = range(b * pages_per_blk, (b + 1) * pages_per_blk)
        s = jnp.concatenate([jnp.dot(qm, k_refs[pg][...].astype(BF16), preferred_element_type=F32)
                             for pg in pages], axis=1)
        gate = jnp.sum(s, axis=1, keepdims=True) * (1.0 / MOBA_BLOCK)
        m_blk = jnp.max(s, axis=1, keepdims=True)
        p = jnp.exp(s - m_blk)
        l_blk = jnp.sum(p, axis=1, keepdims=True)
        p = p.astype(BF16)
        part = None
        for j, pg in enumerate(pages):
            term = _dot_nt(p[:, j * PAGE_SIZE:(j + 1) * PAGE_SIZE], v_refs[pg][...].astype(BF16))
            part = term if part is None else part + term
        part_ref[n] = part
        here = col == n
        g_ref[...] = jnp.where(here, gate, g_ref[...])
        m_ref[...] = jnp.where(here, m_blk, m_ref[...])
        l_ref[...] = jnp.where(here, l_blk, l_ref[...])

    @pl.when(step == pl.num_programs(1) - 1)
    def _():
        sel = _top_blocks(g_ref[...], col, n_blk)
        s_self = jnp.sum(qm.astype(F32) * kn_ref[0].astype(F32), axis=1, keepdims=True)
        m_all = m_ref[...]
        m_fin = jnp.maximum(jnp.max(jnp.where(sel > 0.0, m_all, NEG_INF), axis=1, keepdims=True), s_self)
        w = jnp.where(sel > 0.0, jnp.exp(m_all - m_fin), 0.0)
        w_self = jnp.exp(s_self - m_fin)
        l_fin = jnp.sum(w * l_ref[...], axis=1, keepdims=True) + w_self
        acc = w_self * jnp.broadcast_to(vn_ref[0].astype(F32), (MOBA_HEADS, D_MODEL))

        def add(j, a):
            wj = jnp.sum(jnp.where(col == j, w, 0.0), axis=1, keepdims=True)
            return a + wj * part_ref[j]

        acc = lax.fori_loop(0, n_blk, add, acc)
        accn = acc / l_fin
        row = lax.broadcasted_iota(jnp.int32, accn.shape, 0)
        lane = lax.broadcasted_iota(jnp.int32, accn.shape, 1)
        o = jnp.sum(jnp.where(lane // MOBA_DH == row, accn, 0.0), axis=0, keepdims=True)
        o_ref[0] = o.astype(BF16)


def _moba_decode_call(page_ids, q_s, k_new, v_new, cache_k, cache_v):
    db, n_pages = page_ids.shape
    npg = MOBA_DECODE_PAGES
    n_blk = n_pages * PAGE_SIZE // MOBA_BLOCK
    seq3 = lambda b, n, pt: (b, 0, 0)

    def page(i):
        return pl.BlockSpec((None, D_MODEL, PAGE_SIZE), lambda b, n, pt: (pt[b, npg * n + i], 0, 0))

    grid_spec = pltpu.PrefetchScalarGridSpec(
        num_scalar_prefetch=1,
        grid=(db, n_pages // npg),
        in_specs=[pl.BlockSpec((1, 1, D_MODEL), seq3)] * 3
                 + [page(i) for i in range(npg)] + [page(i) for i in range(npg)],
        out_specs=pl.BlockSpec((1, 1, D_MODEL), seq3),
        scratch_shapes=[pltpu.VMEM((MOBA_HEADS, LANES), F32), pltpu.VMEM((MOBA_HEADS, LANES), F32),
                        pltpu.VMEM((MOBA_HEADS, LANES), F32),
                        pltpu.VMEM((n_blk, MOBA_HEADS, D_MODEL), F32)],
    )
    return pl.pallas_call(
        _moba_decode_kernel,
        grid_spec=grid_spec,
        out_shape=jax.ShapeDtypeStruct((db, 1, D_MODEL), BF16),
        compiler_params=_cparams(("arbitrary", "arbitrary")),
        name="moba_attn_decode",
    )(page_ids, q_s, k_new, v_new, *([cache_k] * npg), *([cache_v] * npg))


def _layer_norm(z, g, b):
    mu = jnp.mean(z, axis=1, keepdims=True)
    zc = z - mu
    var = jnp.mean(zc * zc, axis=1, keepdims=True)
    return zc * lax.rsqrt(var + LN_EPS) * g + b


def _outproj_kernel(o_ref, h_ref, wo_ref, g_ref, b_ref, wr_ref, br_ref,
                    h1_ref, idx_ref, gate_ref, *, alpha):
    mix = jnp.dot(o_ref[...], wo_ref[...], preferred_element_type=F32)
    h1 = _layer_norm(alpha * h_ref[...] + mix, g_ref[...], b_ref[...])
    h1_ref[...] = h1
    logits = jnp.dot(h1, wr_ref[...], preferred_element_type=F32,
                     precision=lax.Precision.HIGHEST) + br_ref[...]
    lane = lax.broadcasted_iota(jnp.int32, logits.shape, 1)
    logits = jnp.where(lane < N_EXPERTS, logits, NEG_INF)
    vals, idxs = [], []
    for _ in range(TOP_K):
        mx = jnp.max(logits, axis=1, keepdims=True)
        ix = jnp.min(jnp.where(logits == mx, lane, LANES), axis=1, keepdims=True)
        vals.append(mx)
        idxs.append(ix)
        logits = jnp.where(lane == ix, NEG_INF, logits)
    es = [jnp.exp(v - vals[0]) for v in vals]
    tot = es[0] + es[1] + es[2] + es[3]
    for j in range(TOP_K):
        idx_ref[:, j:j + 1] = idxs[j]
        gate_ref[:, j:j + 1] = es[j] / tot


def _outproj_call(o_bf, h, wo_bf, ln_g, ln_b, wr_pad, br_pad, alpha):
    n = h.shape[0]
    row = lambda i: (i, 0)
    fix = lambda i: (0, 0)
    return pl.pallas_call(
        functools.partial(_outproj_kernel, alpha=alpha),
        grid=(n // TOK_TILE,),
        in_specs=[pl.BlockSpec((TOK_TILE, D_MODEL), row),
                  pl.BlockSpec((TOK_TILE, D_MODEL), row),
                  pl.BlockSpec((D_MODEL, D_MODEL), fix),
                  pl.BlockSpec((1, D_MODEL), fix),
                  pl.BlockSpec((1, D_MODEL), fix),
                  pl.BlockSpec((D_MODEL, LANES), fix),
                  pl.BlockSpec((1, LANES), fix)],
        out_specs=[pl.BlockSpec((TOK_TILE, D_MODEL), row),
                   pl.BlockSpec((TOK_TILE, TOP_K), row),
                   pl.BlockSpec((TOK_TILE, TOP_K), row)],
        out_shape=[jax.ShapeDtypeStruct((n, D_MODEL), F32),
                   jax.ShapeDtypeStruct((n, TOP_K), jnp.int32),
                   jax.ShapeDtypeStruct((n, TOP_K), F32)],
        compiler_params=_cparams(("arbitrary",)),
        name="outproj_ln_router",
    )(o_bf, h, wo_bf, ln_g, ln_b, wr_pad, br_pad)


def _route_kernel(idx_ref, rank_ref, cnt_ref, run_ref, *, n_tok):
    i = pl.program_id(0)
    t = TOK_TILE

    @pl.when(i == 0)
    def _():
        run_ref[...] = jnp.zeros(run_ref.shape, F32)

    idx = idx_ref[...]
    lane = lax.broadcasted_iota(jnp.int32, (t, LANES), 1)
    live = i * t + lax.broadcasted_iota(jnp.int32, (t, LANES), 0) < n_tok
    hits = [(lane == idx[:, j:j + 1]) & live for j in range(TOP_K)]
    chose = jnp.where(hits[0], 1.0, 0.0)
    for j in range(1, TOP_K):
        chose = chose + jnp.where(hits[j], 1.0, 0.0)
    r = lax.broadcasted_iota(jnp.int32, (t, t), 0)
    c = lax.broadcasted_iota(jnp.int32, (t, t), 1)
    tri = jnp.where(c <= r, 1.0, 0.0).astype(BF16)
    incl = jnp.dot(tri, chose.astype(BF16), preferred_element_type=F32)
    before = run_ref[...] + incl - chose
    for j in range(TOP_K):
        rank_ref[:, j:j + 1] = jnp.sum(jnp.where(hits[j], before, 0.0), axis=1,
                                       keepdims=True).astype(jnp.int32)
    run_ref[...] = run_ref[...] + incl[t - 1:t, :]
    cnt_ref[...] = run_ref[...].astype(jnp.int32)


def _route_call(top_idx, n_tok):
    n = top_idx.shape[0]
    return pl.pallas_call(
        functools.partial(_route_kernel, n_tok=n_tok),
        grid=(n // TOK_TILE,),
        in_specs=[pl.BlockSpec((TOK_TILE, TOP_K), lambda i: (i, 0))],
        out_specs=[pl.BlockSpec((TOK_TILE, TOP_K), lambda i: (i, 0)),
                   pl.BlockSpec((1, LANES), lambda i: (0, 0))],
        out_shape=[jax.ShapeDtypeStruct((n, TOP_K), jnp.int32),
                   jax.ShapeDtypeStruct((1, LANES), jnp.int32)],
        scratch_shapes=[pltpu.VMEM((1, LANES), F32)],
        compiler_params=_cparams(("arbitrary",)),
        name="route_rank",
    )(top_idx)


def _route_tables(top_idx, rank, counts, n_tok, n_pad):
    counts = counts[0, :N_EXPERTS]
    padded = (counts + MOE_TILE - 1) // MOE_TILE * MOE_TILE
    ends = jnp.cumsum(padded)
    start = ends - padded
    n_asg = n_tok * TOP_K
    n_spare = (n_pad - n_tok) * TOP_K
    n_tiles = -(-n_asg // MOE_TILE) + N_EXPERTS + -(-n_spare // MOE_TILE)
    experts = jnp.arange(N_EXPERTS, dtype=jnp.int32)
    dest = jnp.sum(jnp.where(top_idx[..., None] == experts, start, 0), axis=-1) + rank
    flat = jnp.arange(n_pad * TOP_K, dtype=jnp.int32).reshape(n_pad, TOP_K)
    spare = n_tiles * MOE_TILE - n_spare + (flat - n_asg)
    dest = jnp.where(flat < n_asg, dest, spare).astype(jnp.int32)
    tile_start = jnp.arange(n_tiles, dtype=jnp.int32) * MOE_TILE
    tile_e = jnp.minimum(jnp.sum(tile_start[:, None] >= ends[None, :], axis=1), N_EXPERTS - 1).astype(jnp.int32)
    n_used = (ends[-1:] // MOE_TILE).astype(jnp.int32)
    nt = n_pad // TOK_TILE
    by_token = dest.reshape(nt, 1, TOK_TILE * TOP_K)
    by_slot = dest.reshape(nt, TOK_TILE, TOP_K).transpose(0, 2, 1).reshape(nt, 1, TOK_TILE * TOP_K)
    return tile_e, n_used, n_tiles, by_token, by_slot


def _scatter_kernel(dest_ref, h1_ref, xs_in_hbm, xs_hbm, sem):
    del xs_in_hbm

    def row_copy(t, dst_row):
        return pltpu.make_async_copy(h1_ref.at[pl.ds(t, 1)], xs_hbm.at[pl.ds(dst_row, 1)], sem)

    def start(t, c):
        for j in range(TOP_K):
            row_copy(t, dest_ref[0, 0, t * TOP_K + j]).start()
        return c

    lax.fori_loop(0, TOK_TILE, start, 0, unroll=2)

    def wait(t, c):
        for j in range(TOP_K):
            row_copy(t, dest_ref[0, 0, t * TOP_K + j]).wait()
        return c

    lax.fori_loop(0, TOK_TILE, wait, 0, unroll=2)


def _scatter_call(by_token, h1, n_rows):
    nt = h1.shape[0] // TOK_TILE
    return pl.pallas_call(
        _scatter_kernel,
        grid=(nt,),
        in_specs=[pl.BlockSpec((1, 1, TOK_TILE * TOP_K), lambda i: (i, 0, 0), memory_space=pltpu.SMEM),
                  pl.BlockSpec((TOK_TILE, D_MODEL), lambda i: (i, 0)),
                  pl.BlockSpec(memory_space=pl.ANY)],
        out_specs=pl.BlockSpec(memory_space=pl.ANY),
        out_shape=jax.ShapeDtypeStruct((n_rows, D_MODEL), F32),
        scratch_shapes=[pltpu.SemaphoreType.DMA(())],
        input_output_aliases={2: 0},
        compiler_params=_cparams(("arbitrary",)),
        name="moe_scatter_rows",
    )(by_token, h1, jnp.zeros((n_rows, D_MODEL), F32))


def _moe_kernel(te_ref, nu_ref, x_ref, wgu_ref, bgu_ref, wdn_ref, bdn_ref, y_ref, wgu_bf, wdn_bf):
    i = pl.program_id(0)
    live = i < nu_ref[0]
    new_expert = (i == 0) | (te_ref[i] != te_ref[jnp.maximum(i - 1, 0)])

    @pl.when(live & new_expert)
    def _():
        wgu_bf[...] = wgu_ref[0].astype(BF16)
        wdn_bf[...] = wdn_ref[0].astype(BF16)

    @pl.when(live)
    def _():
        x = x_ref[...].astype(BF16)
        h = jnp.dot(x, wgu_bf[...], preferred_element_type=F32) + bgu_ref[0]
        gate = jnp.minimum(h[:, :D_FF], SWIGLU_LIMIT)
        up = jnp.clip(h[:, D_FF:], -SWIGLU_LIMIT, SWIGLU_LIMIT)
        glu = gate * (1.0 / (1.0 + jnp.exp(-SWIGLU_ALPHA * gate)))
        act = ((up + 1.0) * glu).astype(BF16)
        y_ref[...] = jnp.dot(act, wdn_bf[...], preferred_element_type=F32) + bdn_ref[0]

    @pl.when(jnp.logical_not(live))
    def _():
        y_ref[...] = jnp.zeros(y_ref.shape, F32)


def _moe_call(tile_e, n_used, xs, layer, w_gu, b_gu, w_dn, b_dn):
    n_tiles = tile_e.shape[0]

    def tile(i, te, nu):
        return jnp.minimum(i, nu[0] - 1)

    def wmap(i, te, nu):
        return (layer, te[tile(i, te, nu)], 0, 0)

    grid_spec = pltpu.PrefetchScalarGridSpec(
        num_scalar_prefetch=2,
        grid=(n_tiles,),
        in_specs=[pl.BlockSpec((MOE_TILE, D_MODEL), lambda i, te, nu: (tile(i, te, nu), 0)),
                  pl.BlockSpec((None, 1, D_MODEL, 2 * D_FF), wmap),
                  pl.BlockSpec((None, 1, 1, 2 * D_FF), wmap),
                  pl.BlockSpec((None, 1, D_FF, D_MODEL), wmap),
                  pl.BlockSpec((None, 1, 1, D_MODEL), wmap)],
        out_specs=pl.BlockSpec((MOE_TILE, D_MODEL), lambda i, te, nu: (i, 0)),
        scratch_shapes=[pltpu.VMEM((D_MODEL, 2 * D_FF), BF16), pltpu.VMEM((D_FF, D_MODEL), BF16)],
    )
    depth = w_gu.shape[0]
    return pl.pallas_call(
        _moe_kernel,
        grid_spec=grid_spec,
        out_shape=jax.ShapeDtypeStruct((n_tiles * MOE_TILE, D_MODEL), F32),
        compiler_params=_cparams(("arbitrary",), MOE_VMEM_LIMIT),
        name="moe_experts",
    )(tile_e, n_used, xs, w_gu, b_gu.reshape(depth, N_EXPERTS, 1, 2 * D_FF), w_dn,
      b_dn.reshape(depth, N_EXPERTS, 1, D_MODEL))


def _row_copy(src_hbm, dst_vmem, sem, src_row, dst_row):
    return pltpu.make_async_copy(src_hbm.at[pl.ds(src_row, 1)], dst_vmem.at[pl.ds(dst_row, 1)], sem)


def _combine_kernel(pos_ref, gates_ref, h1_ref, g_ref, b_ref, ys_hbm, h2_ref, buf, sem, *, alpha):
    n_rows = TOK_TILE * TOP_K

    def start(a, c):
        _row_copy(ys_hbm, buf, sem, pos_ref[0, 0, a], a).start()
        return c

    lax.fori_loop(0, n_rows, start, 0, unroll=8)

    def wait(a, c):
        _row_copy(ys_hbm, buf, sem, pos_ref[0, 0, a], a).wait()
        return c

    lax.fori_loop(0, n_rows, wait, 0, unroll=8)
    gates = gates_ref[...]
    y = gates[:, 0:1] * buf[0:TOK_TILE, :]
    for j in range(1, TOP_K):
        y += gates[:, j:j + 1] * buf[j * TOK_TILE:(j + 1) * TOK_TILE, :]
    h2_ref[...] = _layer_norm(alpha * h1_ref[...] + y, g_ref[...], b_ref[...])


def _combine_call(pos, gates, h1, ln_g, ln_b, ys, alpha):
    n = h1.shape[0]
    nt = n // TOK_TILE
    row = lambda i: (i, 0)
    fix = lambda i: (0, 0)
    return pl.pallas_call(
        functools.partial(_combine_kernel, alpha=alpha),
        grid=(nt,),
        in_specs=[pl.BlockSpec((1, 1, TOK_TILE * TOP_K), lambda i: (i, 0, 0), memory_space=pltpu.SMEM),
                  pl.BlockSpec((TOK_TILE, TOP_K), row),
                  pl.BlockSpec((TOK_TILE, D_MODEL), row),
                  pl.BlockSpec((1, D_MODEL), fix),
                  pl.BlockSpec((1, D_MODEL), fix),
                  pl.BlockSpec(memory_space=pl.ANY)],
        out_specs=pl.BlockSpec((TOK_TILE, D_MODEL), row),
        out_shape=jax.ShapeDtypeStruct((n, D_MODEL), F32),
        scratch_shapes=[pltpu.VMEM((TOK_TILE * TOP_K, D_MODEL), F32), pltpu.SemaphoreType.DMA(())],
        compiler_params=_cparams(("arbitrary",)),
        name="moe_combine_ln",
    )(pos, gates, h1, ln_g, ln_b, ys)


def kernel(x_prompt, x_sample, cache_k_diff, cache_v_diff, cache_k_moba, cache_v_moba, page_table,
           diff_w_qkv, diff_w_o, diff_lambda_q1, diff_lambda_k1, diff_lambda_q2, diff_lambda_k2,
           diff_subln_g, moba_w_qkv, moba_w_o, ln1_g, ln1_b, ln2_g, ln2_b, moe_w_router,
           moe_b_router, moe_w_gate_up, moe_b_gate_up, moe_w_down, moe_b_down):
    batch, seq, _ = x_prompt.shape
    db = x_sample.shape[0]
    depth = ln1_g.shape[0]
    n_prompt = batch * seq
    n_tok = n_prompt + db
    n_pad = -(-n_tok // TOK_TILE) * TOK_TILE
    past_len = page_table.shape[1] * PAGE_SIZE
    alpha = (2.0 * depth) ** 0.25

    h = jnp.concatenate([x_prompt.reshape(n_prompt, D_MODEL), x_sample.reshape(db, D_MODEL),
                         jnp.zeros((n_pad - n_tok, D_MODEL), F32)], axis=0)
    pos = jnp.concatenate([jnp.tile(jnp.arange(seq, dtype=jnp.int32), batch),
                           jnp.full((n_pad - n_prompt,), past_len, jnp.int32)])
    cos, sa, sb = _rope_tables(pos)
    pad_rows = jnp.zeros((n_pad - n_tok, D_MODEL), BF16)
    sample = slice(n_prompt, n_tok)

    def feature_major(c):
        perm = (0, 1) + tuple(range(3, c.ndim)) + (2,)
        return jnp.transpose(c, perm).reshape(c.shape[0] * c.shape[1], D_MODEL, PAGE_SIZE)

    def token_major(feat, fshape):
        nf = len(fshape)
        return jnp.transpose(feat.reshape((batch,) + fshape + (seq,)), (0, nf + 1) + tuple(range(1, nf + 1)))

    outs = {name: [] for name in ("kd_p", "vd_p", "km_p", "vm_p", "kd_s", "vd_s", "km_s", "vm_s")}
    for i in range(depth):
        j = i // 2
        is_diff = i % 2 == 0
        w_qkv = (diff_w_qkv if is_diff else moba_w_qkv)[j].astype(BF16)
        w_o = (diff_w_o if is_diff else moba_w_o)[j].astype(BF16)
        q_bf, k_bf, v_bf, k_feat, v_out, k_tail, v_tail, kmean = _proj_call(
            h, w_qkv, cos, sa, sb, batch, seq, v_feature_major=not is_diff)
        q_s = q_bf[sample].reshape(db, 1, D_MODEL)
        k_s = k_bf[sample].reshape(db, 1, D_MODEL)
        v_s = v_bf[sample].reshape(db, 1, D_MODEL)
        if is_diff:
            lam_init = 0.8 - 0.6 * math.exp(-0.3 * i)
            lams = [p[j].astype(F32).reshape(1, DIFF_DK)
                    for p in (diff_lambda_q1, diff_lambda_k1, diff_lambda_q2, diff_lambda_k2)]
            g = diff_subln_g[j].astype(F32).reshape(1, LANES)
            pages = page_table + j * cache_k_diff.shape[1]
            o_p = _diff_attn_call(q_bf, k_bf, v_bf, lams, g, batch, seq, lam_init)
            v_pages = cache_v_diff.reshape(-1, PAGE_SIZE * DIFF_HEADS, LANES)
            o_s = _diff_decode_call(pages, q_s, k_s, v_s, feature_major(cache_k_diff), v_pages,
                                    lams, g, lam_init)
            pre = "d"
            kshape, vshape = (DIFF_HEADS, 2, DIFF_DK), (DIFF_HEADS, 2 * DIFF_DK)
            v_prompt = v_out.reshape((batch, seq) + vshape)
        else:
            pages = page_table + j * cache_k_moba.shape[1]
            o_p = _moba_attn_call(q_bf, k_bf, v_bf, kmean.reshape(-1, D_MODEL), batch, seq)
            o_s = _moba_decode_call(pages, q_s, k_s, v_s, feature_major(cache_k_moba),
                                    feature_major(cache_v_moba))
            pre = "m"
            kshape, vshape = (MOBA_HEADS, MOBA_DH), (MOBA_HEADS, MOBA_DH)
            v_prompt = token_major(v_out, vshape)
        outs[f"k{pre}_p"].append(token_major(k_feat, kshape))
        outs[f"v{pre}_p"].append(v_prompt)
        outs[f"k{pre}_s"].append(k_tail[:db].reshape((db, 1) + kshape))
        outs[f"v{pre}_s"].append(v_tail[:db].reshape((db, 1) + vshape))

        o_all = jnp.concatenate([o_p, o_s.reshape(db, D_MODEL), pad_rows], axis=0)
        wr_pad = jnp.zeros((D_MODEL, LANES), F32).at[:, :N_EXPERTS].set(moe_w_router[i].astype(F32))
        br_pad = jnp.zeros((1, LANES), F32).at[0, :N_EXPERTS].set(moe_b_router[i].astype(F32))
        h1, top_idx, gates = _outproj_call(o_all, h, w_o, ln1_g[i].reshape(1, D_MODEL),
                                           ln1_b[i].reshape(1, D_MODEL), wr_pad, br_pad, alpha)
        rank, counts = _route_call(top_idx, n_tok)
        tile_e, n_used, n_tiles, by_token, by_slot = _route_tables(top_idx, rank, counts, n_tok, n_pad)
        xs = _scatter_call(by_token, h1, n_tiles * MOE_TILE)
        ys = _moe_call(tile_e, n_used, xs, i, moe_w_gate_up, moe_b_gate_up, moe_w_down, moe_b_down)
        h = _combine_call(by_slot, gates, h1, ln2_g[i].reshape(1, D_MODEL), ln2_b[i].reshape(1, D_MODEL),
                          ys, alpha)

    y_prompt = h[:n_prompt].reshape(batch, seq, D_MODEL)
    y_sample = h[sample].reshape(db, 1, D_MODEL)
    return (y_prompt, y_sample, jnp.stack(outs["kd_p"]), jnp.stack(outs["vd_p"]),
            jnp.stack(outs["km_p"]), jnp.stack(outs["vm_p"]), jnp.stack(outs["kd_s"]),
            jnp.stack(outs["vd_s"]), jnp.stack(outs["km_s"]), jnp.stack(outs["vm_s"]))
```

```python
import functools
import math

import jax
import jax.numpy as jnp
from jax import lax
from jax.experimental import pallas as pl
from jax.experimental.pallas import tpu as pltpu

F32 = jnp.float32
BF16 = jnp.bfloat16
NEG_INF = float("-inf")

D_MODEL = 1024
PAGE_SIZE = 128
DIFF_HEADS = 8
DIFF_DK = 64
MOBA_HEADS = 16
MOBA_DH = 64
MOBA_BLOCK = 256
MOBA_TOPK = 3
N_EXPERTS = 32
TOP_K = 4
D_FF = D_MODEL
SWIGLU_LIMIT = 7.0
SWIGLU_ALPHA = 1.702
ROPE_THETA = 10000.0
LN_EPS = 1e-5
HEAD_DIM = 64
Q_SCALE = 1.0 / math.sqrt(HEAD_DIM)

LANES = 128
N_COLBLK = D_MODEL // LANES
TOK_TILE = 256
DIFF_TQ = 512
MOE_TILE = 256
DECODE_PAGES = 8
MOBA_DECODE_PAGES = 8
MOBA_KV_BLOCKS = 4
VMEM_LIMIT = 48 * 1024 * 1024
MOE_VMEM_LIMIT = 56 * 1024 * 1024


def _cparams(sem, vmem=VMEM_LIMIT):
    return pltpu.CompilerParams(dimension_semantics=sem, vmem_limit_bytes=vmem)


def _rope_tables(pos):
    inv = ROPE_THETA ** (-jnp.arange(0, HEAD_DIM, 2, dtype=F32) / HEAD_DIM)
    ang = pos.astype(F32)[:, None] * inv[None, :]
    ang = jnp.tile(ang, (1, LANES // (HEAD_DIM // 2)))
    cos, sin = jnp.cos(ang), jnp.sin(ang)
    first = (jnp.arange(LANES) % HEAD_DIM) < HEAD_DIM // 2
    return cos, jnp.where(first, -sin, 0.0), jnp.where(first, 0.0, sin)


def _proj_kernel(x_ref, w_ref, cos_ref, sa_ref, sb_ref,
                 q_ref, kb_ref, vb_ref, kt_ref, vp_ref, ks_ref, vs_ref, km_ref, *, n_prompt_tiles, v_feature_major):
    i = pl.program_id(0)
    xb = x_ref[...].astype(BF16)
    cos, sa, sb = cos_ref[...], sa_ref[...], sb_ref[...]

    def rope(a):
        half = HEAD_DIM // 2
        return a * cos + pltpu.roll(a, LANES - half, 1) * sa + pltpu.roll(a, half, 1) * sb

    aq = jnp.dot(xb, w_ref[:, 0:D_MODEL], preferred_element_type=F32)
    for c in range(N_COLBLK):
        sl = slice(c * LANES, (c + 1) * LANES)
        q_ref[:, sl] = (rope(aq[:, sl]) * Q_SCALE).astype(BF16)
    ak = jnp.dot(xb, w_ref[:, D_MODEL:2 * D_MODEL], preferred_element_type=F32)
    kr = []
    for c in range(N_COLBLK):
        sl = slice(c * LANES, (c + 1) * LANES)
        kr.append(rope(ak[:, sl]))
        kb_ref[:, sl] = kr[c].astype(BF16)
        km_ref[0, :, sl] = jnp.mean(kr[c], axis=0, keepdims=True)
    av = jnp.dot(xb, w_ref[:, 2 * D_MODEL:3 * D_MODEL], preferred_element_type=F32)
    vb_ref[...] = av.astype(BF16)

    @pl.when(i < n_prompt_tiles)
    def _():
        for c in range(N_COLBLK):
            sl = slice(c * LANES, (c + 1) * LANES)
            kt_ref[sl, :] = kr[c].T
            if v_feature_major:
                vp_ref[sl, :] = av[:, sl].T
        if not v_feature_major:
            vp_ref[...] = av

    @pl.when(i == n_prompt_tiles)
    def _():
        ks_ref[...] = jnp.concatenate(kr, axis=1)
        vs_ref[...] = av


def _proj_call(x, w_bf, cos, sa, sb, batch, seq, v_feature_major):
    n = x.shape[0]
    nt = n // TOK_TILE
    tiles_per_seq = seq // TOK_TILE
    npt = batch * tiles_per_seq
    assert nt == npt + 1, "sample tokens and padding must fit one token tile"
    row = lambda i: (i, 0)
    fix = lambda i: (0, 0)

    def feat(i):
        j = jnp.minimum(i, npt - 1)
        return (j // tiles_per_seq, 0, j % tiles_per_seq)

    feat_spec = pl.BlockSpec((None, D_MODEL, TOK_TILE), feat)
    feat_shape = jax.ShapeDtypeStruct((batch, D_MODEL, seq), F32)
    if v_feature_major:
        v_spec, v_shape = feat_spec, feat_shape
    else:
        v_spec = pl.BlockSpec((TOK_TILE, D_MODEL), lambda i: (jnp.minimum(i, npt - 1), 0))
        v_shape = jax.ShapeDtypeStruct((batch * seq, D_MODEL), F32)
    return pl.pallas_call(
        functools.partial(_proj_kernel, n_prompt_tiles=npt, v_feature_major=v_feature_major),
        grid=(nt,),
        in_specs=[pl.BlockSpec((TOK_TILE, D_MODEL), row),
                  pl.BlockSpec((D_MODEL, 3 * D_MODEL), fix),
                  pl.BlockSpec((TOK_TILE, LANES), row),
                  pl.BlockSpec((TOK_TILE, LANES), row),
                  pl.BlockSpec((TOK_TILE, LANES), row)],
        out_specs=[pl.BlockSpec((TOK_TILE, D_MODEL), row),
                   pl.BlockSpec((TOK_TILE, D_MODEL), row),
                   pl.BlockSpec((TOK_TILE, D_MODEL), row),
                   feat_spec, v_spec,
                   pl.BlockSpec((TOK_TILE, D_MODEL), fix),
                   pl.BlockSpec((TOK_TILE, D_MODEL), fix),
                   pl.BlockSpec((1, 1, D_MODEL), lambda i: (i, 0, 0))],
        out_shape=[jax.ShapeDtypeStruct((n, D_MODEL), BF16),
                   jax.ShapeDtypeStruct((n, D_MODEL), BF16),
                   jax.ShapeDtypeStruct((n, D_MODEL), BF16),
                   feat_shape, v_shape,
                   jax.ShapeDtypeStruct((TOK_TILE, D_MODEL), F32),
                   jax.ShapeDtypeStruct((TOK_TILE, D_MODEL), F32),
                   jax.ShapeDtypeStruct((nt, 1, D_MODEL), F32)],
        compiler_params=_cparams(("arbitrary",)),
        name="qkv_rope",
    )(x, w_bf, cos, sa, sb)


def _half_masks(q):
    lane = lax.broadcasted_iota(jnp.int32, q.shape, 1)
    qf = q.astype(F32)
    return (jnp.where(lane < HEAD_DIM, qf, 0.0).astype(q.dtype),
            jnp.where(lane >= HEAD_DIM, qf, 0.0).astype(q.dtype))


def _dot_nt(a, b):
    return lax.dot_general(a, b, (((1,), (1,)), ((), ())), preferred_element_type=F32)


def _with_ones(vt):
    return jnp.concatenate([vt, jnp.ones(vt.shape, vt.dtype)], axis=1)


def _flash_update(c, s, v_aug, m_ref, acc_ref):
    m_prev = m_ref[c]
    m_new = jnp.maximum(m_prev, jnp.max(s, axis=1, keepdims=True))
    alpha = jnp.exp(m_prev - m_new)
    p = jnp.concatenate([jnp.exp(s[:, j * LANES:(j + 1) * LANES] - m_new)
                         for j in range(s.shape[1] // LANES)], axis=1)
    acc_ref[c] = (jnp.concatenate([alpha, alpha], axis=1) * acc_ref[c]
                  + jnp.dot(p.astype(BF16), v_aug, preferred_element_type=F32))
    m_ref[c] = m_new


def _flash_result(c, acc_ref):
    acc = acc_ref[c]
    return acc[:, :LANES] / acc[:, LANES:]


def _diff_lambda(lq1, lk1, lq2, lk2, lam_init):
    return (jnp.exp(jnp.sum(lq1 * lk1, axis=1, keepdims=True))
            - jnp.exp(jnp.sum(lq2 * lk2, axis=1, keepdims=True)) + lam_init)


def _diff_attn_kernel(q_ref, k_ref, v_ref, lq1_ref, lk1_ref, lq2_ref, lk2_ref, g_ref,
                      o_ref, m_ref, acc_ref, sa_ref, sb_ref, *, lam_init):
    t = DIFF_TQ
    qi = pl.program_id(2)
    q_maps = _half_masks(q_ref[...])
    m_ref[...] = jnp.full(m_ref.shape, NEG_INF, F32)
    acc_ref[...] = jnp.zeros(acc_ref.shape, F32)

    def scores(kj, s_ref):
        kt = k_ref[pl.ds(pl.multiple_of(kj * t, t), t), :]
        for c in range(2):
            s_ref[c] = _dot_nt(q_maps[c], kt)

    def consume(kj, s_ref, causal):
        v_aug = _with_ones(v_ref[pl.ds(pl.multiple_of(kj * t, t), t), :])
        for c in range(2):
            s = s_ref[c]
            if causal:
                row = lax.broadcasted_iota(jnp.int32, s.shape, 0)
                col = lax.broadcasted_iota(jnp.int32, s.shape, 1)
                s = jnp.where(col <= row, s, NEG_INF)
            _flash_update(c, s, v_aug, m_ref, acc_ref)

    scores(0, sa_ref)

    def pair(mi, carry):
        kj = 2 * mi
        scores(kj + 1, sb_ref)
        consume(kj, sa_ref, False)
        scores(kj + 2, sa_ref)
        consume(kj + 1, sb_ref, False)
        return carry

    lax.fori_loop(0, qi // 2, pair, 0)

    @pl.when(qi % 2 == 0)
    def _():
        consume(qi, sa_ref, True)

    @pl.when(qi % 2 == 1)
    def _():
        scores(qi, sb_ref)
        consume(qi - 1, sa_ref, False)
        consume(qi, sb_ref, True)

    lam = _diff_lambda(lq1_ref[...], lk1_ref[...], lq2_ref[...], lk2_ref[...], lam_init)
    o = _flash_result(0, acc_ref) - lam * _flash_result(1, acc_ref)
    ms = jnp.mean(o * o, axis=1, keepdims=True)
    o = o * lax.rsqrt(ms + LN_EPS) * g_ref[...] * (1.0 - lam_init)
    o_ref[...] = o.astype(BF16)


def _diff_attn_call(q_bf, k_bf, v_bf, lams, g, batch, seq, lam_init):
    t = DIFF_TQ
    nq = seq // t
    qmap = lambda b, h, i: (b * nq + i, h)
    kvmap = lambda b, h, i: (b, h)
    fix = lambda b, h, i: (0, 0)
    return pl.pallas_call(
        functools.partial(_diff_attn_kernel, lam_init=lam_init),
        grid=(batch, DIFF_HEADS, nq),
        in_specs=[pl.BlockSpec((t, LANES), qmap),
                  pl.BlockSpec((seq, LANES), kvmap),
                  pl.BlockSpec((seq, LANES), kvmap)]
                 + [pl.BlockSpec((1, DIFF_DK), fix)] * 4
                 + [pl.BlockSpec((1, LANES), fix)],
        out_specs=pl.BlockSpec((t, LANES), qmap),
        out_shape=jax.ShapeDtypeStruct((batch * seq, D_MODEL), BF16),
        scratch_shapes=[pltpu.VMEM((2, t, LANES), F32), pltpu.VMEM((2, t, 2 * LANES), F32),
                        pltpu.VMEM((2, t, t), F32), pltpu.VMEM((2, t, t), F32)],
        compiler_params=_cparams(("arbitrary", "arbitrary", "arbitrary")),
        name="diff_attn_prompt",
    )(q_bf, k_bf, v_bf, *lams, g)


def _row_query(q, width):
    rows = D_MODEL // width
    lane = lax.broadcasted_iota(jnp.int32, (rows, D_MODEL), 1)
    row = lax.broadcasted_iota(jnp.int32, (rows, D_MODEL), 0)
    qb = jnp.broadcast_to(q.astype(F32), (rows, D_MODEL))
    return jnp.where(lane // width == row, qb, 0.0).astype(q.dtype)


def _map_major_query(q):
    rows = 2 * DIFF_HEADS
    lane = lax.broadcasted_iota(jnp.int32, (rows, D_MODEL), 1)
    row = lax.broadcasted_iota(jnp.int32, (rows, D_MODEL), 0)
    qb = jnp.broadcast_to(q.astype(F32), (rows, D_MODEL))
    chunk = 2 * (row % DIFF_HEADS) + row // DIFF_HEADS
    return jnp.where(lane // DIFF_DK == chunk, qb, 0.0).astype(q.dtype)


def _diff_decode_kernel(pt_ref, q_ref, kn_ref, vn_ref, e_ref, *rest, lam_init):
    np_ = DECODE_PAGES
    k_refs, v_refs = rest[:np_], rest[np_:2 * np_]
    lq1_ref, lk1_ref, lq2_ref, lk2_ref, g_ref, o_ref, m_ref, l_ref, acc_ref = rest[2 * np_:]
    g = pl.program_id(1)
    rows = 2 * DIFF_HEADS
    qm = _map_major_query(q_ref[0])

    @pl.when(g == 0)
    def _():
        s_self = jnp.sum(qm.astype(F32) * kn_ref[0].astype(F32), axis=1, keepdims=True)
        m_ref[...] = s_self
        l_ref[...] = jnp.ones(l_ref.shape, F32)
        vn = vn_ref[0].astype(F32)
        acc_ref[...] = jnp.concatenate([vn, vn], axis=0)

    s = jnp.concatenate([jnp.dot(qm, k_refs[i][...].astype(BF16), preferred_element_type=F32)
                         for i in range(np_)], axis=1)
    m_prev = m_ref[...]
    m_new = jnp.maximum(m_prev, jnp.max(s, axis=1, keepdims=True))
    alpha = jnp.exp(m_prev - m_new)
    p = jnp.exp(s - m_new)
    l_ref[...] = alpha * l_ref[...] + jnp.sum(p, axis=1, keepdims=True)
    p = p.astype(BF16)
    p_rows = jnp.concatenate([p[:, i * PAGE_SIZE:(i + 1) * PAGE_SIZE] for i in range(np_)], axis=0)
    spread = jnp.dot(p_rows, e_ref[...], preferred_element_type=F32)
    srow = lax.broadcasted_iota(jnp.int32, spread.shape, 0)
    slane = lax.broadcasted_iota(jnp.int32, spread.shape, 1)
    spread = jnp.where(slane % DIFF_HEADS == srow % DIFF_HEADS, spread, 0.0).astype(BF16)
    pv = None
    for i in range(np_):
        part = jnp.dot(spread[i * rows:(i + 1) * rows, :], v_refs[i][...].astype(BF16),
                       preferred_element_type=F32)
        pv = part if pv is None else pv + part
    acc_ref[...] = alpha * acc_ref[...] + pv
    m_ref[...] = m_new

    @pl.when(g == pl.num_programs(1) - 1)
    def _():
        accn = acc_ref[...] / l_ref[...]
        lam = _diff_lambda(lq1_ref[...], lk1_ref[...], lq2_ref[...], lk2_ref[...], lam_init)
        o = accn[0:DIFF_HEADS] - lam * accn[DIFF_HEADS:rows]
        ms = jnp.mean(o * o, axis=1, keepdims=True)
        o_ref[0] = (o * lax.rsqrt(ms + LN_EPS) * g_ref[...] * (1.0 - lam_init)).astype(BF16)


def _diff_decode_call(page_ids, q_s, k_new, v_new, cache_k, cache_v, lams, g, lam_init):
    db, n_pages = page_ids.shape
    np_ = DECODE_PAGES
    seq3 = lambda b, g_, pt: (b, 0, 0)
    fix = lambda b, g_, pt: (0, 0)
    tok = jnp.arange(PAGE_SIZE, dtype=jnp.int32)[:, None]
    rowid = jnp.arange(PAGE_SIZE * DIFF_HEADS, dtype=jnp.int32)[None, :]
    spread_mat = (rowid // DIFF_HEADS == tok).astype(BF16)

    def page(i):
        return pl.BlockSpec((None, D_MODEL, PAGE_SIZE), lambda b, g_, pt: (pt[b, g_ * np_ + i], 0, 0))

    grid_spec = pltpu.PrefetchScalarGridSpec(
        num_scalar_prefetch=1,
        grid=(db, n_pages // np_),
        in_specs=[pl.BlockSpec((1, 1, D_MODEL), seq3)] * 2
                 + [pl.BlockSpec((1, DIFF_HEADS, LANES), seq3)]
                 + [pl.BlockSpec((PAGE_SIZE, D_MODEL), fix)]
                 + [page(i) for i in range(np_)] + [page(i) for i in range(np_)]
                 + [pl.BlockSpec((1, DIFF_DK), fix)] * 4
                 + [pl.BlockSpec((1, LANES), fix)],
        out_specs=pl.BlockSpec((1, DIFF_HEADS, LANES), seq3),
        scratch_shapes=[pltpu.VMEM((2 * DIFF_HEADS, 1), F32), pltpu.VMEM((2 * DIFF_HEADS, 1), F32),
                        pltpu.VMEM((2 * DIFF_HEADS, LANES), F32)],
    )
    return pl.pallas_call(
        functools.partial(_diff_decode_kernel, lam_init=lam_init),
        grid_spec=grid_spec,
        out_shape=jax.ShapeDtypeStruct((db, DIFF_HEADS, LANES), BF16),
        compiler_params=_cparams(("arbitrary", "arbitrary")),
        name="diff_attn_decode",
    )(page_ids, q_s, k_new, v_new.reshape(db, DIFF_HEADS, LANES), spread_mat,
      *([cache_k] * np_), *([cache_v] * np_), *lams, g)


def _top_blocks(g, blk, n_valid):
    g = jnp.where(blk < n_valid, g, NEG_INF)
    blk_f = blk.astype(F32)
    n_valid_f = jnp.float32(n_valid) if isinstance(n_valid, int) else n_valid.astype(F32)
    sel = jnp.zeros(g.shape, F32)
    for _ in range(MOBA_TOPK):
        mx = jnp.max(g, axis=1, keepdims=True)
        ix = jnp.min(jnp.where(g == mx, blk_f, float(g.shape[1])), axis=1, keepdims=True)
        pick = blk_f == ix
        sel = jnp.where(pick & (ix < n_valid_f), 1.0, sel)
        g = jnp.where(pick, NEG_INF, g)
    return sel


def _moba_attn_kernel(q_ref, k_ref, v_ref, km_ref, o_ref, m_ref, acc_ref, sel_ref, sa_ref, sb_ref):
    t = MOBA_BLOCK
    nb = MOBA_KV_BLOCKS
    w = nb * t
    qi = pl.program_id(2)
    n_chunks = (qi + nb - 1) // nb
    last_chunk = k_ref.shape[0] // w - 1
    q_heads = _half_masks(q_ref[...])
    km = km_ref[...].astype(BF16)
    blk = lax.broadcasted_iota(jnp.int32, (t, km.shape[0]), 1)

    def scores(kj, s_ref):
        kj = jnp.minimum(kj, last_chunk)
        kt = k_ref[pl.ds(pl.multiple_of(kj * w, w), w), :]
        for c in range(2):
            s_ref[c] = _dot_nt(q_heads[c], kt)

    def consume(kj, s_ref):
        v_aug = _with_ones(v_ref[pl.ds(pl.multiple_of(kj * w, w), w), :])
        for c in range(2):
            s = s_ref[c]
            parts = []
            for i in range(nb):
                chosen = jnp.max(jnp.where(blk == kj * nb + i, sel_ref[c], 0.0), axis=1, keepdims=True)
                parts.append(jnp.where(chosen > 0.0, s[:, i * t:(i + 1) * t], NEG_INF))
            _flash_update(c, jnp.concatenate(parts, axis=1), v_aug, m_ref, acc_ref)

    scores(0, sa_ref)
    for c in range(2):
        sel_ref[c] = _top_blocks(_dot_nt(q_heads[c], km), blk, qi)
    m_ref[...] = jnp.full(m_ref.shape, NEG_INF, F32)
    acc_ref[...] = jnp.zeros(acc_ref.shape, F32)

    start = pl.multiple_of(qi * t, t)
    kt = k_ref[pl.ds(start, t), :]
    v_aug = _with_ones(v_ref[pl.ds(start, t), :])
    row = lax.broadcasted_iota(jnp.int32, (t, t), 0)
    col = lax.broadcasted_iota(jnp.int32, (t, t), 1)
    for c in range(2):
        s = jnp.where(col <= row, _dot_nt(q_heads[c], kt), NEG_INF)
        _flash_update(c, s, v_aug, m_ref, acc_ref)

    def pair(mi, carry):
        kj = 2 * mi
        scores(kj + 1, sb_ref)
        consume(kj, sa_ref)
        scores(kj + 2, sa_ref)
        consume(kj + 1, sb_ref)
        return carry

    lax.fori_loop(0, n_chunks // 2, pair, 0)

    @pl.when(n_chunks % 2 == 1)
    def _():
        consume(n_chunks - 1, sa_ref)

    lane = lax.broadcasted_iota(jnp.int32, (t, LANES), 1)
    o = jnp.where(lane < HEAD_DIM, _flash_result(0, acc_ref), _flash_result(1, acc_ref))
    o_ref[...] = o.astype(BF16)


def _moba_attn_call(q_bf, k_bf, v_bf, kmean, batch, seq):
    t = MOBA_BLOCK
    nq = seq // t
    qmap = lambda b, h, i: (b * nq + i, h)
    kvmap = lambda b, h, i: (b, h)
    return pl.pallas_call(
        _moba_attn_kernel,
        grid=(batch, MOBA_HEADS // 2, nq),
        in_specs=[pl.BlockSpec((t, LANES), qmap),
                  pl.BlockSpec((seq, LANES), kvmap),
                  pl.BlockSpec((seq, LANES), kvmap),
                  pl.BlockSpec((nq, LANES), kvmap)],
        out_specs=pl.BlockSpec((t, LANES), qmap),
        out_shape=jax.ShapeDtypeStruct((batch * seq, D_MODEL), BF16),
        scratch_shapes=[pltpu.VMEM((2, t, LANES), F32), pltpu.VMEM((2, t, 2 * LANES), F32),
                        pltpu.VMEM((2, t, nq), F32),
                        pltpu.VMEM((2, t, MOBA_KV_BLOCKS * t), F32),
                        pltpu.VMEM((2, t, MOBA_KV_BLOCKS * t), F32)],
        compiler_params=_cparams(("arbitrary", "arbitrary", "arbitrary")),
        name="moba_attn_prompt",
    )(q_bf, k_bf, v_bf, kmean)


def _moba_decode_kernel(pt_ref, q_ref, kn_ref, vn_ref, *rest):
    npg = MOBA_DECODE_PAGES
    pages_per_blk = MOBA_BLOCK // PAGE_SIZE
    k_refs, v_refs = rest[:npg], rest[npg:2 * npg]
    o_ref, g_ref, m_ref, l_ref, part_ref = rest[2 * npg:]
    step = pl.program_id(1)
    blks = npg // pages_per_blk
    n_blk = pl.num_programs(1) * blks
    qm = _row_query(q_ref[0], MOBA_DH)
    col = lax.broadcasted_iota(jnp.int32, g_ref.shape, 1)

    @pl.when(step == 0)
    def _():
        g_ref[...] = jnp.full(g_ref.shape, NEG_INF, F32)
        m_ref[...] = jnp.zeros(m_ref.shape, F32)
        l_ref[...] = jnp.zeros(l_ref.shape, F32)

    for b in range(blks):
        n = step * blks + b
        pages = range(b * pages_per_blk, (b + 1) * pages_per_blk)
        s = jnp.concatenate([jnp.dot(qm, k_refs[pg][...].astype(BF16), preferred_element_type=F32)
                             for pg in pages], axis=1)
        gate = jnp.sum(s, axis=1, keepdims=True) * (1.0 / MOBA_BLOCK)
        m_blk = jnp.max(s, axis=1, keepdims=True)
        p = jnp.exp(s - m_blk)
        l_blk = jnp.sum(p, axis=1, keepdims=True)
        p = p.astype(BF16)
        part = None
        for j, pg in enumerate(pages):
            term = _dot_nt(p[:, j * PAGE_SIZE:(j + 1) * PAGE_SIZE], v_refs[pg][...].astype(BF16))
            part = term if part is None else part + term
        part_ref[n] = part
        here = col == n
        g_ref[...] = jnp.where(here, gate, g_ref[...])
        m_ref[...] = jnp.where(here, m_blk, m_ref[...])
        l_ref[...] = jnp.where(here, l_blk, l_ref[...])

    @pl.when(step == pl.num_programs(1) - 1)
    def _():
        sel = _top_blocks(g_ref[...], col, n_blk)
        s_self = jnp.sum(qm.astype(F32) * kn_ref[0].astype(F32), axis=1, keepdims=True)
        m_all = m_ref[...]
        m_fin = jnp.maximum(jnp.max(jnp.where(sel > 0.0, m_all, NEG_INF), axis=1, keepdims=True), s_self)
        w = jnp.where(sel > 0.0, jnp.exp(m_all - m_fin), 0.0)
        w_self = jnp.exp(s_self - m_fin)
        l_fin = jnp.sum(w * l_ref[...], axis=1, keepdims=True) + w_self
        acc = w_self * jnp.broadcast_to(vn_ref[0].astype(F32), (MOBA_HEADS, D_MODEL))

        def add(j, a):
            wj = jnp.sum(jnp.where(col == j, w, 0.0), axis=1, keepdims=True)
            return a + wj * part_ref[j]

        acc = lax.fori_loop(0, n_blk, add, acc)
        accn = acc / l_fin
        row = lax.broadcasted_iota(jnp.int32, accn.shape, 0)
        lane = lax.broadcasted_iota(jnp.int32, accn.shape, 1)
        o = jnp.sum(jnp.where(lane // MOBA_DH == row, accn, 0.0), axis=0, keepdims=True)
        o_ref[0] = o.astype(BF16)


def _moba_decode_call(page_ids, q_s, k_new, v_new, cache_k, cache_v):
    db, n_pages = page_ids.shape
    npg = MOBA_DECODE_PAGES
    n_blk = n_pages * PAGE_SIZE // MOBA_BLOCK
    seq3 = lambda b, n, pt: (b, 0, 0)

    def page(i):
        return pl.BlockSpec((None, D_MODEL, PAGE_SIZE), lambda b, n, pt: (pt[b, npg * n + i], 0, 0))

    grid_spec = pltpu.PrefetchScalarGridSpec(
        num_scalar_prefetch=1,
        grid=(db, n_pages // npg),
        in_specs=[pl.BlockSpec((1, 1, D_MODEL), seq3)] * 3
                 + [page(i) for i in range(npg)] + [page(i) for i in range(npg)],
        out_specs=pl.BlockSpec((1, 1, D_MODEL), seq3),
        scratch_shapes=[pltpu.VMEM((MOBA_HEADS, LANES), F32), pltpu.VMEM((MOBA_HEADS, LANES), F32),
                        pltpu.VMEM((MOBA_HEADS, LANES), F32),
                        pltpu.VMEM((n_blk, MOBA_HEADS, D_MODEL), F32)],
    )
    return pl.pallas_call(
        _moba_decode_kernel,
        grid_spec=grid_spec,
        out_shape=jax.ShapeDtypeStruct((db, 1, D_MODEL), BF16),
        compiler_params=_cparams(("arbitrary", "arbitrary")),
        name="moba_attn_decode",
    )(page_ids, q_s, k_new, v_new, *([cache_k] * npg), *([cache_v] * npg))


def _layer_norm(z, g, b):
    mu = jnp.mean(z, axis=1, keepdims=True)
    zc = z - mu
    var = jnp.mean(zc * zc, axis=1, keepdims=True)
    return zc * lax.rsqrt(var + LN_EPS) * g + b


def _outproj_kernel(o_ref, h_ref, wo_ref, g_ref, b_ref, wr_ref, br_ref,
                    h1_ref, idx_ref, gate_ref, *, alpha):
    mix = jnp.dot(o_ref[...], wo_ref[...], preferred_element_type=F32)
    h1 = _layer_norm(alpha * h_ref[...] + mix, g_ref[...], b_ref[...])
    h1_ref[...] = h1
    wr = wr_ref[...]
    w_hi = wr.astype(BF16)
    w_lo = (wr - w_hi.astype(F32)).astype(BF16)
    h_hi = h1.astype(BF16)
    h_lo = (h1 - h_hi.astype(F32)).astype(BF16)
    logits = (jnp.dot(h_hi, w_hi, preferred_element_type=F32)
              + (jnp.dot(h_hi, w_lo, preferred_element_type=F32)
                 + jnp.dot(h_lo, w_hi, preferred_element_type=F32))) + br_ref[...]
    lane = lax.broadcasted_iota(jnp.int32, logits.shape, 1)
    logits = jnp.where(lane < N_EXPERTS, logits, NEG_INF)
    lane_f = lane.astype(F32)
    vals, idxs = [], []
    for _ in range(TOP_K):
        mx = jnp.max(logits, axis=1, keepdims=True)
        ix = jnp.min(jnp.where(logits == mx, lane_f, float(LANES)), axis=1, keepdims=True)
        vals.append(mx)
        idxs.append(ix)
        logits = jnp.where(lane_f == ix, NEG_INF, logits)
    es = [jnp.exp(v - vals[0]) for v in vals]
    tot = es[0] + es[1] + es[2] + es[3]
    for j in range(TOP_K):
        idx_ref[:, j:j + 1] = idxs[j].astype(jnp.int32)
        gate_ref[:, j:j + 1] = es[j] / tot


def _outproj_call(o_bf, h, wo_bf, ln_g, ln_b, wr_pad, br_pad, alpha):
    n = h.shape[0]
    row = lambda i: (i, 0)
    fix = lambda i: (0, 0)
    return pl.pallas_call(
        functools.partial(_outproj_kernel, alpha=alpha),
        grid=(n // TOK_TILE,),
        in_specs=[pl.BlockSpec((TOK_TILE, D_MODEL), row),
                  pl.BlockSpec((TOK_TILE, D_MODEL), row),
                  pl.BlockSpec((D_MODEL, D_MODEL), fix),
                  pl.BlockSpec((1, D_MODEL), fix),
                  pl.BlockSpec((1, D_MODEL), fix),
                  pl.BlockSpec((D_MODEL, LANES), fix),
                  pl.BlockSpec((1, LANES), fix)],
        out_specs=[pl.BlockSpec((TOK_TILE, D_MODEL), row),
                   pl.BlockSpec((TOK_TILE, TOP_K), row),
                   pl.BlockSpec((TOK_TILE, TOP_K), row)],
        out_shape=[jax.ShapeDtypeStruct((n, D_MODEL), F32),
                   jax.ShapeDtypeStruct((n, TOP_K), jnp.int32),
                   jax.ShapeDtypeStruct((n, TOP_K), F32)],
        compiler_params=_cparams(("arbitrary",)),
        name="outproj_ln_router",
    )(o_bf, h, wo_bf, ln_g, ln_b, wr_pad, br_pad)


def _route_kernel(idx_ref, rank_ref, cnt_ref, run_ref, *, n_tok):
    i = pl.program_id(0)
    t = TOK_TILE

    @pl.when(i == 0)
    def _():
        run_ref[...] = jnp.zeros(run_ref.shape, F32)

    idx = idx_ref[...]
    lane = lax.broadcasted_iota(jnp.int32, (t, LANES), 1)
    live = i * t + lax.broadcasted_iota(jnp.int32, (t, LANES), 0) < n_tok
    hits = [(lane == idx[:, j:j + 1]) & live for j in range(TOP_K)]
    chose = jnp.where(hits[0], 1.0, 0.0)
    for j in range(1, TOP_K):
        chose = chose + jnp.where(hits[j], 1.0, 0.0)
    r = lax.broadcasted_iota(jnp.int32, (t, t), 0)
    c = lax.broadcasted_iota(jnp.int32, (t, t), 1)
    tri = jnp.where(c <= r, 1.0, 0.0).astype(BF16)
    incl = jnp.dot(tri, chose.astype(BF16), preferred_element_type=F32)
    before = run_ref[...] + incl - chose
    for j in range(TOP_K):
        rank_ref[:, j:j + 1] = jnp.sum(jnp.where(hits[j], before, 0.0), axis=1,
                                       keepdims=True).astype(jnp.int32)
    run_ref[...] = run_ref[...] + incl[t - 1:t, :]
    cnt_ref[...] = run_ref[...].astype(jnp.int32)


def _route_call(top_idx, n_tok):
    n = top_idx.shape[0]
    return pl.pallas_call(
        functools.partial(_route_kernel, n_tok=n_tok),
        grid=(n // TOK_TILE,),
        in_specs=[pl.BlockSpec((TOK_TILE, TOP_K), lambda i: (i, 0))],
        out_specs=[pl.BlockSpec((TOK_TILE, TOP_K), lambda i: (i, 0)),
                   pl.BlockSpec((1, LANES), lambda i: (0, 0))],
        out_shape=[jax.ShapeDtypeStruct((n, TOP_K), jnp.int32),
                   jax.ShapeDtypeStruct((1, LANES), jnp.int32)],
        scratch_shapes=[pltpu.VMEM((1, LANES), F32)],
        compiler_params=_cparams(("arbitrary",)),
        name="route_rank",
    )(top_idx)


def _route_tables(top_idx, rank, counts, n_tok, n_pad):
    counts = counts[0, :N_EXPERTS]
    padded = (counts + MOE_TILE - 1) // MOE_TILE * MOE_TILE
    ends = jnp.cumsum(padded)
    start = ends - padded
    n_asg = n_tok * TOP_K
    n_spare = (n_pad - n_tok) * TOP_K
    n_tiles = -(-n_asg // MOE_TILE) + N_EXPERTS + -(-n_spare // MOE_TILE)
    experts = jnp.arange(N_EXPERTS, dtype=jnp.int32)
    dest = jnp.sum(jnp.where(top_idx[..., None] == experts, start, 0), axis=-1) + rank
    flat = jnp.arange(n_pad * TOP_K, dtype=jnp.int32).reshape(n_pad, TOP_K)
    spare = n_tiles * MOE_TILE - n_spare + (flat - n_asg)
    dest = jnp.where(flat < n_asg, dest, spare).astype(jnp.int32)
    tile_start = jnp.arange(n_tiles, dtype=jnp.int32) * MOE_TILE
    tile_e = jnp.minimum(jnp.sum(tile_start[:, None] >= ends[None, :], axis=1), N_EXPERTS - 1).astype(jnp.int32)
    n_used = (ends[-1:] // MOE_TILE).astype(jnp.int32)
    nt = n_pad // TOK_TILE
    by_token = dest.reshape(nt, 1, TOK_TILE * TOP_K)
    by_slot = dest.reshape(nt, TOK_TILE, TOP_K).transpose(0, 2, 1).reshape(nt, 1, TOK_TILE * TOP_K)
    return tile_e, n_used, n_tiles, by_token, by_slot


def _scatter_kernel(dest_ref, h1_ref, xs_in_hbm, xs_hbm, sem):
    del xs_in_hbm

    def row_copy(t, dst_row):
        return pltpu.make_async_copy(h1_ref.at[pl.ds(t, 1)], xs_hbm.at[pl.ds(dst_row, 1)], sem)

    def start(t, c):
        for j in range(TOP_K):
            row_copy(t, dest_ref[0, 0, t * TOP_K + j]).start()
        return c

    lax.fori_loop(0, TOK_TILE, start, 0, unroll=2)

    def wait(t, c):
        for j in range(TOP_K):
            row_copy(t, dest_ref[0, 0, t * TOP_K + j]).wait()
        return c

    lax.fori_loop(0, TOK_TILE, wait, 0, unroll=2)


def _scatter_call(by_token, h1, n_rows):
    nt = h1.shape[0] // TOK_TILE
    return pl.pallas_call(
        _scatter_kernel,
        grid=(nt,),
        in_specs=[pl.BlockSpec((1, 1, TOK_TILE * TOP_K), lambda i: (i, 0, 0), memory_space=pltpu.SMEM),
                  pl.BlockSpec((TOK_TILE, D_MODEL), lambda i: (i, 0)),
                  pl.BlockSpec(memory_space=pl.ANY)],
        out_specs=pl.BlockSpec(memory_space=pl.ANY),
        out_shape=jax.ShapeDtypeStruct((n_rows, D_MODEL), F32),
        scratch_shapes=[pltpu.SemaphoreType.DMA(())],
        input_output_aliases={2: 0},
        compiler_params=_cparams(("arbitrary",)),
        name="moe_scatter_rows",
    )(by_token, h1, jnp.zeros((n_rows, D_MODEL), F32))


def _moe_kernel(te_ref, nu_ref, x_ref, wgu_ref, bgu_ref, wdn_ref, bdn_ref, y_ref, wgu_bf, wdn_bf):
    i = pl.program_id(0)
    live = i < nu_ref[0]
    new_expert = (i == 0) | (te_ref[i] != te_ref[jnp.maximum(i - 1, 0)])

    @pl.when(live & new_expert)
    def _():
        wgu_bf[...] = wgu_ref[0].astype(BF16)
        wdn_bf[...] = wdn_ref[0].astype(BF16)

    @pl.when(live)
    def _():
        x = x_ref[...].astype(BF16)
        h = jnp.dot(x, wgu_bf[...], preferred_element_type=F32) + bgu_ref[0]
        gate = jnp.minimum(h[:, :D_FF], SWIGLU_LIMIT)
        up = jnp.clip(h[:, D_FF:], -SWIGLU_LIMIT, SWIGLU_LIMIT)
        glu = gate * (1.0 / (1.0 + jnp.exp(-SWIGLU_ALPHA * gate)))
        act = ((up + 1.0) * glu).astype(BF16)
        y_ref[...] = jnp.dot(act, wdn_bf[...], preferred_element_type=F32) + bdn_ref[0]

    @pl.when(jnp.logical_not(live))
    def _():
        y_ref[...] = jnp.zeros(y_ref.shape, F32)


def _moe_call(tile_e, n_used, xs, layer, w_gu, b_gu, w_dn, b_dn):
    n_tiles = tile_e.shape[0]

    def tile(i, te, nu):
        return jnp.minimum(i, nu[0] - 1)

    def wmap(i, te, nu):
        return (layer, te[tile(i, te, nu)], 0, 0)

    grid_spec = pltpu.PrefetchScalarGridSpec(
        num_scalar_prefetch=2,
        grid=(n_tiles,),
        in_specs=[pl.BlockSpec((MOE_TILE, D_MODEL), lambda i, te, nu: (tile(i, te, nu), 0)),
                  pl.BlockSpec((None, 1, D_MODEL, 2 * D_FF), wmap),
                  pl.BlockSpec((None, 1, 1, 2 * D_FF), wmap),
                  pl.BlockSpec((None, 1, D_FF, D_MODEL), wmap),
                  pl.BlockSpec((None, 1, 1, D_MODEL), wmap)],
        out_specs=pl.BlockSpec((MOE_TILE, D_MODEL), lambda i, te, nu: (i, 0)),
        scratch_shapes=[pltpu.VMEM((D_MODEL, 2 * D_FF), BF16), pltpu.VMEM((D_FF, D_MODEL), BF16)],
    )
    depth = w_gu.shape[0]
    return pl.pallas_call(
        _moe_kernel,
        grid_spec=grid_spec,
        out_shape=jax.ShapeDtypeStruct((n_tiles * MOE_TILE, D_MODEL), F32),
        compiler_params=_cparams(("arbitrary",), MOE_VMEM_LIMIT),
        name="moe_experts",
    )(tile_e, n_used, xs, w_gu, b_gu.reshape(depth, N_EXPERTS, 1, 2 * D_FF), w_dn,
      b_dn.reshape(depth, N_EXPERTS, 1, D_MODEL))


def _row_copy(src_hbm, dst_vmem, sem, src_row, dst_row):
    return pltpu.make_async_copy(src_hbm.at[pl.ds(src_row, 1)], dst_vmem.at[pl.ds(dst_row, 1)], sem)


def _combine_kernel(pos_ref, gates_ref, h1_ref, g_ref, b_ref, ys_hbm, h2_ref, buf, sem, *, alpha):
    n_rows = TOK_TILE * TOP_K

    def start(a, c):
        _row_copy(ys_hbm, buf, sem, pos_ref[0, 0, a], a).start()
        return c

    lax.fori_loop(0, n_rows, start, 0, unroll=8)

    def wait(a, c):
        _row_copy(ys_hbm, buf, sem, pos_ref[0, 0, a], a).wait()
        return c

    lax.fori_loop(0, n_rows, wait, 0, unroll=8)
    gates = gates_ref[...]
    y = gates[:, 0:1] * buf[0:TOK_TILE, :]
    for j in range(1, TOP_K):
        y += gates[:, j:j + 1] * buf[j * TOK_TILE:(j + 1) * TOK_TILE, :]
    h2_ref[...] = _layer_norm(alpha * h1_ref[...] + y, g_ref[...], b_ref[...])


def _combine_call(pos, gates, h1, ln_g, ln_b, ys, alpha):
    n = h1.shape[0]
    nt = n // TOK_TILE
    row = lambda i: (i, 0)
    fix = lambda i: (0, 0)
    return pl.pallas_call(
        functools.partial(_combine_kernel, alpha=alpha),
        grid=(nt,),
        in_specs=[pl.BlockSpec((1, 1, TOK_TILE * TOP_K), lambda i: (i, 0, 0), memory_space=pltpu.SMEM),
                  pl.BlockSpec((TOK_TILE, TOP_K), row),
                  pl.BlockSpec((TOK_TILE, D_MODEL), row),
                  pl.BlockSpec((1, D_MODEL), fix),
                  pl.BlockSpec((1, D_MODEL), fix),
                  pl.BlockSpec(memory_space=pl.ANY)],
        out_specs=pl.BlockSpec((TOK_TILE, D_MODEL), row),
        out_shape=jax.ShapeDtypeStruct((n, D_MODEL), F32),
        scratch_shapes=[pltpu.VMEM((TOK_TILE * TOP_K, D_MODEL), F32), pltpu.SemaphoreType.DMA(())],
        compiler_params=_cparams(("arbitrary",)),
        name="moe_combine_ln",
    )(pos, gates, h1, ln_g, ln_b, ys)


def kernel(x_prompt, x_sample, cache_k_diff, cache_v_diff, cache_k_moba, cache_v_moba, page_table,
           diff_w_qkv, diff_w_o, diff_lambda_q1, diff_lambda_k1, diff_lambda_q2, diff_lambda_k2,
           diff_subln_g, moba_w_qkv, moba_w_o, ln1_g, ln1_b, ln2_g, ln2_b, moe_w_router,
           moe_b_router, moe_w_gate_up, moe_b_gate_up, moe_w_down, moe_b_down):
    batch, seq, _ = x_prompt.shape
    db = x_sample.shape[0]
    depth = ln1_g.shape[0]
    n_prompt = batch * seq
    n_tok = n_prompt + db
    n_pad = -(-n_tok // TOK_TILE) * TOK_TILE
    past_len = page_table.shape[1] * PAGE_SIZE
    alpha = (2.0 * depth) ** 0.25

    h = jnp.concatenate([x_prompt.reshape(n_prompt, D_MODEL), x_sample.reshape(db, D_MODEL),
                         jnp.zeros((n_pad - n_tok, D_MODEL), F32)], axis=0)
    pos = jnp.concatenate([jnp.tile(jnp.arange(seq, dtype=jnp.int32), batch),
                           jnp.full((n_pad - n_prompt,), past_len, jnp.int32)])
    cos, sa, sb = _rope_tables(pos)
    pad_rows = jnp.zeros((n_pad - n_tok, D_MODEL), BF16)
    sample = slice(n_prompt, n_tok)

    def feature_major(c):
        perm = (0, 1) + tuple(range(3, c.ndim)) + (2,)
        return jnp.transpose(c, perm).reshape(c.shape[0] * c.shape[1], D_MODEL, PAGE_SIZE)

    def token_major(feat, fshape):
        nf = len(fshape)
        return jnp.transpose(feat.reshape((batch,) + fshape + (seq,)), (0, nf + 1) + tuple(range(1, nf + 1)))

    outs = {name: [] for name in ("kd_p", "vd_p", "km_p", "vm_p", "kd_s", "vd_s", "km_s", "vm_s")}
    for i in range(depth):
        j = i // 2
        is_diff = i % 2 == 0
        w_qkv = (diff_w_qkv if is_diff else moba_w_qkv)[j].astype(BF16)
        w_o = (diff_w_o if is_diff else moba_w_o)[j].astype(BF16)
        q_bf, k_bf, v_bf, k_feat, v_out, k_tail, v_tail, kmean = _proj_call(
            h, w_qkv, cos, sa, sb, batch, seq, v_feature_major=not is_diff)
        q_s = q_bf[sample].reshape(db, 1, D_MODEL)
        k_s = k_bf[sample].reshape(db, 1, D_MODEL)
        v_s = v_bf[sample].reshape(db, 1, D_MODEL)
        if is_diff:
            lam_init = 0.8 - 0.6 * math.exp(-0.3 * i)
            lams = [p[j].astype(F32).reshape(1, DIFF_DK)
                    for p in (diff_lambda_q1, diff_lambda_k1, diff_lambda_q2, diff_lambda_k2)]
            g = diff_subln_g[j].astype(F32).reshape(1, LANES)
            pages = page_table + j * cache_k_diff.shape[1]
            o_p = _diff_attn_call(q_bf, k_bf, v_bf, lams, g, batch, seq, lam_init)
            v_pages = cache_v_diff.reshape(-1, PAGE_SIZE * DIFF_HEADS, LANES)
            o_s = _diff_decode_call(pages, q_s, k_s, v_s, feature_major(cache_k_diff), v_pages,
                                    lams, g, lam_init)
            pre = "d"
            kshape, vshape = (DIFF_HEADS, 2, DIFF_DK), (DIFF_HEADS, 2 * DIFF_DK)
            v_prompt = v_out.reshape((batch, seq) + vshape)
        else:
            pages = page_table + j * cache_k_moba.shape[1]
            o_p = _moba_attn_call(q_bf, k_bf, v_bf, kmean.reshape(-1, D_MODEL), batch, seq)
            o_s = _moba_decode_call(pages, q_s, k_s, v_s, feature_major(cache_k_moba),
                                    feature_major(cache_v_moba))
            pre = "m"
            kshape, vshape = (MOBA_HEADS, MOBA_DH), (MOBA_HEADS, MOBA_DH)
            v_prompt = token_major(v_out, vshape)
        outs[f"k{pre}_p"].append(token_major(k_feat, kshape))
        outs[f"v{pre}_p"].append(v_prompt)
        outs[f"k{pre}_s"].append(k_tail[:db].reshape((db, 1) + kshape))
        outs[f"v{pre}_s"].append(v_tail[:db].reshape((db, 1) + vshape))

        o_all = jnp.concatenate([o_p, o_s.reshape(db, D_MODEL), pad_rows], axis=0)
        wr_pad = jnp.zeros((D_MODEL, LANES), F32).at[:, :N_EXPERTS].set(moe_w_router[i].astype(F32))
        br_pad = jnp.zeros((1, LANES), F32).at[0, :N_EXPERTS].set(moe_b_router[i].astype(F32))
        h1, top_idx, gates = _outproj_call(o_all, h, w_o, ln1_g[i].reshape(1, D_MODEL),
                                           ln1_b[i].reshape(1, D_MODEL), wr_pad, br_pad, alpha)
        rank, counts = _route_call(top_idx, n_tok)
        tile_e, n_used, n_tiles, by_token, by_slot = _route_tables(top_idx, rank, counts, n_tok, n_pad)
        xs = _scatter_call(by_token, h1, n_tiles * MOE_TILE)
        ys = _moe_call(tile_e, n_used, xs, i, moe_w_gate_up, moe_b_gate_up, moe_w_down, moe_b_down)
        h = _combine_call(by_slot, gates, h1, ln2_g[i].reshape(1, D_MODEL), ln2_b[i].reshape(1, D_MODEL),
                          ys, alpha)

    y_prompt = h[:n_prompt].reshape(batch, seq, D_MODEL)
    y_sample = h[sample].reshape(db, 1, D_MODEL)
    return (y_prompt, y_sample, jnp.stack(outs["kd_p"]), jnp.stack(outs["vd_p"]),
            jnp.stack(outs["km_p"]), jnp.stack(outs["vm_p"]), jnp.stack(outs["kd_s"]),
            jnp.stack(outs["vd_s"]), jnp.stack(outs["km_s"]), jnp.stack(outs["vm_s"]))
```
